```python
import math, functools
import jax, jax.numpy as jnp
from jax import lax
import numpy as np

D_MODEL = 1024
BATCH = 1
SEQ = 16384
DEPTH = 1
DEC_BATCH = 128
DEC_SEQ = 1
PAST_LEN = 8192
PAGE_SIZE = 128

HEAD_DIM = 64
FOX_HEADS = D_MODEL // (2 * HEAD_DIM)
DIFF_HEADS = D_MODEL // (4 * HEAD_DIM)
FOX_WIDTH = FOX_HEADS * HEAD_DIM
DIFF_WIDTH = DIFF_HEADS * 2 * HEAD_DIM
MIX_WIDTH = FOX_WIDTH + DIFF_WIDTH
N_IN = 3 * FOX_WIDTH + FOX_HEADS + 3 * DIFF_WIDTH
D_FF = ((8 * D_MODEL // 3 + 127) // 128) * 128
CONV_WIDTH = 3
ROPE_THETA = 10000.0
Q_BLOCK = 128
ALPHA = (2 * DEPTH) ** 0.25
BETA = (8 * DEPTH) ** -0.25
FORGET_BIAS_INIT = 4.0

kernel_name = "hybrid_fox_diffattn_convffn_step"


def _layer_norm(x, g, b, eps=1e-5):
    xf = x.astype(jnp.float32)
    mu = jnp.mean(xf, axis=-1, keepdims=True)
    var = jnp.mean(jnp.square(xf - mu), axis=-1, keepdims=True)
    y = (xf - mu) * lax.rsqrt(var + eps)
    return (y * g.astype(jnp.float32) + b.astype(jnp.float32)).astype(x.dtype)


def _rms_norm(x, g, eps=1e-6):
    xf = x.astype(jnp.float32)
    y = xf * lax.rsqrt(jnp.mean(jnp.square(xf), axis=-1, keepdims=True) + eps)
    return (y * g.astype(jnp.float32)).astype(x.dtype)


def _rotary(x, pos):
    half = x.shape[-1] // 2
    inv = ROPE_THETA ** (-jnp.arange(half, dtype=jnp.float32) / half)
    ang = pos.astype(jnp.float32)[:, None] * inv[None, :]
    cos = jnp.cos(ang)[None, :, None, :]
    sin = jnp.sin(ang)[None, :, None, :]
    x1 = x[..., :half].astype(jnp.float32)
    x2 = x[..., half:].astype(jnp.float32)
    return jnp.concatenate([x1 * cos - x2 * sin, x1 * sin + x2 * cos], axis=-1).astype(x.dtype)


def _adaln(c, w, b):
    mod = jax.nn.silu(c) @ w + b
    return [m[:, None, :] for m in jnp.split(mod, 6, axis=-1)]


def _project(h, pos, w_in, b_f):
    B, T = h.shape[0], h.shape[1]
    z = h @ w_in
    cuts = np.cumsum([FOX_WIDTH, FOX_WIDTH, FOX_WIDTH, FOX_HEADS, DIFF_WIDTH, DIFF_WIDTH]).tolist()
    qf, kf, vf, fl, qd, kd, vd = jnp.split(z, cuts, axis=-1)
    qf = qf.reshape(B, T, FOX_HEADS, HEAD_DIM)
    kf = kf.reshape(B, T, FOX_HEADS, HEAD_DIM)
    vf = vf.reshape(B, T, FOX_HEADS, HEAD_DIM)
    logf = jax.nn.log_sigmoid(fl.astype(jnp.float32) + b_f.astype(jnp.float32))
    qd = _rotary(qd.reshape(B, T, 2 * DIFF_HEADS, HEAD_DIM), pos).reshape(B, T, DIFF_HEADS, 2, HEAD_DIM)
    kd = _rotary(kd.reshape(B, T, 2 * DIFF_HEADS, HEAD_DIM), pos).reshape(B, T, DIFF_HEADS, 2, HEAD_DIM)
    vd = vd.reshape(B, T, DIFF_HEADS, 2 * HEAD_DIM)
    return qf, kf, vf, logf, qd, kd, vd


def _fox_attend(q, k, v, cq, ck, q_pos, k_pos):
    s = jnp.einsum('bqhd,bkhd->bhqk', q, k).astype(jnp.float32) * (HEAD_DIM ** -0.5)
    s = s + jnp.swapaxes(cq, 1, 2)[..., :, None] - jnp.swapaxes(ck, 1, 2)[..., None, :]
    mask = k_pos[None, :] <= q_pos[:, None]
    p = jax.nn.softmax(jnp.where(mask, s, -jnp.inf), axis=-1)
    return jnp.einsum('bhqk,bkhd->bqhd', p.astype(v.dtype), v)


def _diff_attend(q, k, v, lam, q_pos, k_pos):
    s = jnp.einsum('bqhmd,bkhmd->bhmqk', q, k).astype(jnp.float32) * (HEAD_DIM ** -0.5)
    mask = k_pos[None, :] <= q_pos[:, None]
    p = jax.nn.softmax(jnp.where(mask, s, -jnp.inf), axis=-1)
    a = p[:, :, 0] - lam.astype(jnp.float32) * p[:, :, 1]
    return jnp.einsum('bhqk,bkhe->bqhe', a.astype(v.dtype), v)


def _prompt_attention(qf, kf, vf, logf, qd, kd, vd, lam):
    B = qf.shape[0]
    pos = jnp.arange(SEQ)
    C = jnp.cumsum(logf, axis=1)

    def block(i):
        s0 = i * Q_BLOCK
        sl = lambda a: lax.dynamic_slice_in_dim(a, s0, Q_BLOCK, axis=1)
        qpos = s0 + jnp.arange(Q_BLOCK)
        of = _fox_attend(sl(qf), kf, vf, sl(C), C, qpos, pos)
        od = _diff_attend(sl(qd), kd, vd, lam, qpos, pos)
        return of, od

    of, od = lax.map(block, jnp.arange(SEQ // Q_BLOCK))
    of = jnp.moveaxis(of, 0, 1).reshape(B, SEQ, FOX_HEADS, HEAD_DIM)
    od = jnp.moveaxis(od, 0, 1).reshape(B, SEQ, DIFF_HEADS, 2 * HEAD_DIM)
    return of, od


def _sample_attention(qf, kf, vf, logf, qd, kd, vd, lam, page_table, pool_fk, pool_fv, pool_fl, pool_dk, pool_dv, layer):
    S = qf.shape[1]
    kpos = jnp.arange(PAST_LEN + S)
    qpos = PAST_LEN + jnp.arange(S)

    def one(args):
        pages, qf1, kf1, vf1, lf1, qd1, kd1, vd1 = args
        past = lambda pool, tail: pool[layer, pages].reshape((PAST_LEN,) + tail)
        kf_all = jnp.concatenate([past(pool_fk, (FOX_HEADS, HEAD_DIM)), kf1], axis=0)
        vf_all = jnp.concatenate([past(pool_fv, (FOX_HEADS, HEAD_DIM)), vf1], axis=0)
        C = jnp.cumsum(jnp.concatenate([past(pool_fl, (FOX_HEADS,)).astype(jnp.float32), lf1], axis=0), axis=0)
        kd_all = jnp.concatenate([past(pool_dk, (DIFF_HEADS, 2, HEAD_DIM)), kd1], axis=0)
        vd_all = jnp.concatenate([past(pool_dv, (DIFF_HEADS, 2 * HEAD_DIM)), vd1], axis=0)
        of = _fox_attend(qf1[None], kf_all[None], vf_all[None], C[None, PAST_LEN:], C[None], qpos, kpos)[0]
        od = _diff_attend(qd1[None], kd_all[None], vd_all[None], lam, qpos, kpos)[0]
        return of, od

    return lax.map(one, (page_table, qf, kf, vf, logf, qd, kd, vd))


def _merge(of, od, subln_g, w_o, lambda_init):
    B, T = of.shape[0], of.shape[1]
    od = _rms_norm(od, subln_g) * (1.0 - lambda_init)
    o = jnp.concatenate([of.reshape(B, T, FOX_WIDTH), od.reshape(B, T, DIFF_WIDTH)], axis=-1)
    return o @ w_o


def _conv_ffn(h, prev, w_gate, w_up, conv_w, conv_b, w_down):
    T = h.shape[1]
    g = h @ w_gate
    u = h @ w_up
    gcat = jnp.concatenate([prev.astype(g.dtype), g], axis=1)
    gc = conv_b + sum(conv_w[j] * gcat[:, j:j + T] for j in range(CONV_WIDTH))
    y = (jax.nn.silu(gc) * u) @ w_down
    return y, gcat[:, -(CONV_WIDTH - 1):]


def _layer(x, c, pos, attend, conv_prev, lam, lambda_init, w_ada, b_ada, w_in, b_f, subln_g, w_o,
           ln_a_g, ln_a_b, w_gate, w_up, conv_w, conv_b, w_down, ln_f_g, ln_f_b):
    sh_a, sc_a, g_a, sh_f, sc_f, g_f = _adaln(c, w_ada, b_ada)
    h = x * (1.0 + sc_a) + sh_a
    qf, kf, vf, logf, qd, kd, vd = _project(h, pos, w_in, b_f)
    of, od = attend(qf, kf, vf, logf, qd, kd, vd, lam)
    x = _layer_norm(ALPHA * x + g_a * _merge(of, od, subln_g, w_o, lambda_init), ln_a_g, ln_a_b)
    h = x * (1.0 + sc_f) + sh_f
    f, conv_new = _conv_ffn(h, conv_prev, w_gate, w_up, conv_w, conv_b, w_down)
    x = _layer_norm(ALPHA * x + g_f * f, ln_f_g, ln_f_b)
    return x, (kf, vf, logf.astype(x.dtype), kd, vd, conv_new)


def setup_inputs(seed: int = 0) -> dict:
    key = jax.random.key(seed)
    ks = jax.random.split(key, 40)
    n_pages = PAST_LEN // PAGE_SIZE
    n_pool = (DEC_BATCH * n_pages * 5) // 4
    f32 = jnp.float32
    nrm = lambda k, shape, s=1.0: jax.random.normal(k, shape, f32) * s
    col_scale = np.ones((N_IN,), np.float32)
    col_scale[2 * FOX_WIDTH:3 * FOX_WIDTH] = BETA
    v0 = 3 * FOX_WIDTH + FOX_HEADS + 2 * DIFF_WIDTH
    col_scale[v0:v0 + DIFF_WIDTH] = BETA
    perm = jax.random.permutation(ks[0], n_pool)[:DEC_BATCH * n_pages]
    return {
        "x_prompt": nrm(ks[1], (BATCH, SEQ, D_MODEL)),
        "x_sample": nrm(ks[2], (DEC_BATCH, DEC_SEQ, D_MODEL)),
        "cache_fox_k": nrm(ks[3], (DEPTH, n_pool, PAGE_SIZE, FOX_HEADS, HEAD_DIM)),
        "cache_fox_v": nrm(ks[4], (DEPTH, n_pool, PAGE_SIZE, FOX_HEADS, HEAD_DIM)),
        "cache_fox_logf": jax.nn.log_sigmoid(FORGET_BIAS_INIT + nrm(ks[5], (DEPTH, n_pool, PAGE_SIZE, FOX_HEADS))),
        "cache_diff_k": nrm(ks[6], (DEPTH, n_pool, PAGE_SIZE, DIFF_HEADS, 2, HEAD_DIM)),
        "cache_diff_v": nrm(ks[7], (DEPTH, n_pool, PAGE_SIZE, DIFF_HEADS, 2 * HEAD_DIM)),
        "state_conv": nrm(ks[8], (DEPTH, DEC_BATCH, CONV_WIDTH - 1, D_FF)),
        "page_table": perm.reshape(DEC_BATCH, n_pages).astype(jnp.int32),
        "c_prompt": nrm(ks[9], (BATCH, D_MODEL)),
        "c_sample": nrm(ks[10], (DEC_BATCH, D_MODEL)),
        "w_ada": nrm(ks[11], (DEPTH, D_MODEL, 6 * D_MODEL), D_MODEL ** -0.5),
        "b_ada": nrm(ks[12], (DEPTH, 6 * D_MODEL), 0.02),
        "w_in": nrm(ks[13], (DEPTH, D_MODEL, N_IN), D_MODEL ** -0.5) * jnp.asarray(col_scale),
        "b_fgate": FORGET_BIAS_INIT + nrm(ks[14], (DEPTH, FOX_HEADS), 0.1),
        "lambda_q1": nrm(ks[15], (DEPTH, HEAD_DIM), 0.1),
        "lambda_k1": nrm(ks[16], (DEPTH, HEAD_DIM), 0.1),
        "lambda_q2": nrm(ks[17], (DEPTH, HEAD_DIM), 0.1),
        "lambda_k2": nrm(ks[18], (DEPTH, HEAD_DIM), 0.1),
        "subln_g": 1.0 + nrm(ks[19], (DEPTH, 2 * HEAD_DIM), 0.02),
        "w_o": nrm(ks[20], (DEPTH, MIX_WIDTH, D_MODEL), MIX_WIDTH ** -0.5 * BETA),
        "ln_a_g": 1.0 + nrm(ks[21], (DEPTH, D_MODEL), 0.02),
        "ln_a_b": nrm(ks[22], (DEPTH, D_MODEL), 0.02),
        "w_gate": nrm(ks[23], (DEPTH, D_MODEL, D_FF), D_MODEL ** -0.5),
        "w_up": nrm(ks[24], (DEPTH, D_MODEL, D_FF), D_MODEL ** -0.5 * BETA),
        "conv_w": nrm(ks[25], (DEPTH, CONV_WIDTH, D_FF), CONV_WIDTH ** -0.5),
        "conv_b": nrm(ks[26], (DEPTH, D_FF), 0.02),
        "w_down": nrm(ks[27], (DEPTH, D_FF, D_MODEL), D_FF ** -0.5 * BETA),
        "ln_f_g": 1.0 + nrm(ks[28], (DEPTH, D_MODEL), 0.02),
        "ln_f_b": nrm(ks[29], (DEPTH, D_MODEL), 0.02),
    }


def reference(x_prompt, x_sample, cache_fox_k, cache_fox_v, cache_fox_logf, cache_diff_k, cache_diff_v,
              state_conv, page_table, c_prompt, c_sample, w_ada, b_ada, w_in, b_fgate,
              lambda_q1, lambda_k1, lambda_q2, lambda_k2, subln_g, w_o, ln_a_g, ln_a_b,
              w_gate, w_up, conv_w, conv_b, w_down, ln_f_g, ln_f_b):
    xp, xs = x_prompt, x_sample
    pos_p = jnp.arange(SEQ)
    pos_s = PAST_LEN + jnp.arange(DEC_SEQ)
    conv_zero = jnp.zeros((BATCH, CONV_WIDTH - 1, D_FF), x_prompt.dtype)
    new_p = [[] for _ in range(6)]
    new_s = [[] for _ in range(6)]
    for l in range(DEPTH):
        lambda_init = 0.8 - 0.6 * math.exp(-0.3 * l)
        lam = (jnp.exp(jnp.sum(lambda_q1[l].astype(jnp.float32) * lambda_k1[l].astype(jnp.float32)))
               - jnp.exp(jnp.sum(lambda_q2[l].astype(jnp.float32) * lambda_k2[l].astype(jnp.float32)))
               + lambda_init)
        params = (w_ada[l], b_ada[l], w_in[l], b_fgate[l], subln_g[l], w_o[l], ln_a_g[l], ln_a_b[l],
                  w_gate[l], w_up[l], conv_w[l], conv_b[l], w_down[l], ln_f_g[l], ln_f_b[l])
        attend_s = functools.partial(_sample_attention, page_table=page_table, pool_fk=cache_fox_k,
                                     pool_fv=cache_fox_v, pool_fl=cache_fox_logf, pool_dk=cache_diff_k,
                                     pool_dv=cache_diff_v, layer=l)
        xp, rows_p = _layer(xp, c_prompt, pos_p, _prompt_attention, conv_zero, lam, lambda_init, *params)
        xs, rows_s = _layer(xs, c_sample, pos_s, attend_s, state_conv[l], lam, lambda_init, *params)
        for i in range(6):
            new_p[i].append(rows_p[i])
            new_s[i].append(rows_s[i])
    fox_k_p, fox_v_p, fox_logf_p, diff_k_p, diff_v_p, conv_p = [jnp.stack(a) for a in new_p]
    fox_k_s, fox_v_s, fox_logf_s, diff_k_s, diff_v_s, conv_s = [jnp.stack(a) for a in new_s]
    return (xp, xs, fox_k_p, fox_v_p, fox_logf_p, diff_k_p, diff_v_p, conv_p,
            fox_k_s, fox_v_s, fox_logf_s, diff_k_s, diff_v_s, conv_s)
```

```python
import functools
import math

import jax
import jax.numpy as jnp
from jax import lax
from jax.experimental import pallas as pl
from jax.experimental.pallas import tpu as pltpu

F32, BF16 = jnp.float32, jnp.bfloat16

HEAD_DIM = 64
ROPE_THETA = 10000.0
CONV_WIDTH = 3
NEG = -1e30
LANES = 128
SUBLANES = 8
VMEM_LIMIT_BYTES = 56 * 1024 * 1024

PROJ_TILE = 512
ATTN_TILE = 512
POST_TILE = 256
ADALN_TILE = 1024
DECODE_PAGES = 8

FOX_SLOT = 128
VF_SLOT = HEAD_DIM + SUBLANES
VD_SLOT = 2 * HEAD_DIM + SUBLANES

_NT = (((1,), (1,)), ((), ()))


def _params(*sem):
    return pltpu.CompilerParams(dimension_semantics=sem, vmem_limit_bytes=VMEM_LIMIT_BYTES)


def _whole(shape):
    return pl.BlockSpec(shape, lambda *_: (0,) * len(shape), pipeline_mode=pl.Buffered(1))


def _whole_out(shape):
    return pl.BlockSpec(shape, lambda *_: (0,) * len(shape))


def _log_sigmoid(x):
    return jnp.minimum(x, 0.0) - jnp.log1p(jnp.exp(-jnp.abs(x)))


def _split3(x):
    hi = x.astype(BF16)
    r = x - hi.astype(F32)
    mid = r.astype(BF16)
    lo = (r - mid.astype(F32)).astype(BF16)
    return hi, mid, lo


def _dot3(parts, rhs):
    return sum(jnp.dot(p, rhs, preferred_element_type=F32) for p in parts)


def _layer_norm(x, g, b, eps=1e-5):
    mu = jnp.mean(x, axis=-1, keepdims=True)
    xc = x - mu
    var = jnp.mean(xc * xc, axis=-1, keepdims=True)
    return xc * lax.rsqrt(var + eps) * g + b


def _lambda(l1q, l1k, l2q, l2k, lambda_init):
    s1 = jnp.sum(l1q[...] * l1k[...], axis=-1, keepdims=True)
    s2 = jnp.sum(l2q[...] * l2k[...], axis=-1, keepdims=True)
    return jnp.exp(s1) - jnp.exp(s2) + lambda_init


def _adaln_body(c_ref, w_ref, b_ref, o_ref):
    a = jax.nn.silu(c_ref[...]).astype(BF16)
    o_ref[...] = jnp.dot(a, w_ref[...].astype(BF16), preferred_element_type=F32) + b_ref[...]


def _adaln(c_all, w, b):
    rows, d = c_all.shape
    n = w.shape[1]
    tn = min(ADALN_TILE, n)
    return pl.pallas_call(
        _adaln_body, grid=(n // tn,),
        in_specs=[_whole((rows, d)), pl.BlockSpec((d, tn), lambda j: (0, j)), pl.BlockSpec((1, tn), lambda j: (0, j))],
        out_specs=pl.BlockSpec((rows, tn), lambda j: (0, j)),
        out_shape=jax.ShapeDtypeStruct((rows, n), F32),
        compiler_params=_params("arbitrary"), name="adaln")(c_all, w, b)


def _rotary_t(x_t, cos_t, sin_t):
    half = HEAD_DIM // 2
    out = []
    for g in range(x_t.shape[0] // HEAD_DIM):
        x1 = x_t[g * HEAD_DIM: g * HEAD_DIM + half]
        x2 = x_t[g * HEAD_DIM + half: (g + 1) * HEAD_DIM]
        out += [x1 * cos_t - x2 * sin_t, x1 * sin_t + x2 * cos_t]
    return jnp.concatenate(out, axis=0)


def _proj_p_body(x_ref, sh_ref, sc_ref, wt_ref, bf_ref, cos_ref, sin_ref,
                 kft_o, vft_o, lft_o, kdt_o, vd_o, qat_o, ka_o, vat_o, qdat_o, kda_o, vdat_o,
                 carry_ref, *, fw, fh, dw, scale):
    tm = x_ref.shape[0]

    @pl.when(pl.program_id(0) == 0)
    def _():
        carry_ref[...] = jnp.zeros_like(carry_ref)

    hb = (x_ref[...] * (1.0 + sc_ref[...]) + sh_ref[...]).astype(BF16)
    z_t = lax.dot_general(wt_ref[...], hb, _NT, preferred_element_type=F32)
    o_fl = 3 * fw
    o_qd = o_fl + fh
    qf_t = z_t[0:fw] * scale
    kf_t = z_t[fw:2 * fw]
    vf_t = z_t[2 * fw:3 * fw]
    lf_t = _log_sigmoid(z_t[o_fl:o_fl + fh] + bf_ref[...])
    qd_t = _rotary_t(z_t[o_qd:o_qd + dw], cos_ref[...], sin_ref[...]) * scale
    kd_t = _rotary_t(z_t[o_qd + dw:o_qd + 2 * dw], cos_ref[...], sin_ref[...])
    vd_t = z_t[o_qd + 2 * dw:o_qd + 3 * dw]

    j = lax.broadcasted_iota(jnp.int32, (tm, tm), 0)
    t = lax.broadcasted_iota(jnp.int32, (tm, tm), 1)
    c_t = _dot3(_split3(lf_t), (j <= t).astype(BF16)) + carry_ref[:, 0:1]
    carry_ref[...] = jnp.broadcast_to(c_t[:, tm - 1:tm], carry_ref.shape)

    kft_o[...] = kf_t
    vft_o[...] = vf_t
    lft_o[...] = lf_t
    kdt_o[...] = kd_t
    vd_o[...] = vd_t.T

    chi, cmid, clo = (p.astype(F32) for p in _split3(c_t))
    tail_rows = FOX_SLOT - HEAD_DIM
    ri = lax.broadcasted_iota(jnp.int32, (tail_rows, tm), 0)
    qa = []
    for h in range(fh):
        tail = jnp.where(ri == 0, chi[h:h + 1], jnp.where(ri == 1, cmid[h:h + 1], jnp.where(
            ri == 2, clo[h:h + 1], jnp.where(ri < 6, 1.0, 0.0))))
        qa += [qf_t[h * HEAD_DIM:(h + 1) * HEAD_DIM], tail]
    qat_o[...] = jnp.concatenate(qa, axis=0).astype(BF16)

    kf = kf_t.T
    c_n = c_t.T
    nhi, nmid, nlo = (p.astype(F32) for p in _split3(c_n))
    li = lax.broadcasted_iota(jnp.int32, (tm, tail_rows), 1)
    ka = []
    for h in range(fh):
        tail = jnp.where(li < 3, 1.0, jnp.where(li == 3, -nhi[:, h:h + 1], jnp.where(
            li == 4, -nmid[:, h:h + 1], jnp.where(li == 5, -nlo[:, h:h + 1], 0.0))))
        ka += [kf[:, h * HEAD_DIM:(h + 1) * HEAD_DIM], tail]
    ka_o[...] = jnp.concatenate(ka, axis=1).astype(BF16)

    ones_row = (lax.broadcasted_iota(jnp.int32, (SUBLANES, tm), 0) == 0).astype(F32)
    va = []
    for h in range(fh):
        va += [vf_t[h * HEAD_DIM:(h + 1) * HEAD_DIM], ones_row]
    vat_o[...] = jnp.concatenate(va, axis=0).astype(BF16)

    zq = jnp.zeros((FOX_SLOT - HEAD_DIM, tm), F32)
    qda = []
    for g in range(dw // HEAD_DIM):
        qda += [qd_t[g * HEAD_DIM:(g + 1) * HEAD_DIM], zq]
    qdat_o[...] = jnp.concatenate(qda, axis=0).astype(BF16)

    kd = kd_t.T
    zk = jnp.zeros((tm, FOX_SLOT - HEAD_DIM), F32)
    kda = []
    for g in range(dw // HEAD_DIM):
        kda += [kd[:, g * HEAD_DIM:(g + 1) * HEAD_DIM], zk]
    kda_o[...] = jnp.concatenate(kda, axis=1).astype(BF16)

    vda = []
    for h in range(dw // (2 * HEAD_DIM)):
        vda += [vd_t[h * 2 * HEAD_DIM:(h + 1) * 2 * HEAD_DIM], ones_row]
    vdat_o[...] = jnp.concatenate(vda, axis=0).astype(BF16)


def _proj_prompt(x, sh, sc, w_t, b_f, cos_t, sin_t, *, fw, fh, dw):
    seq, d = x.shape
    n_in = w_t.shape[0]
    tm = min(PROJ_TILE, seq)
    dh = dw // (2 * HEAD_DIM)
    ng = dw // HEAD_DIM
    col = lambda rows: pl.BlockSpec((rows, tm), lambda i: (0, i))
    row = lambda cols: pl.BlockSpec((tm, cols), lambda i: (i, 0))
    out_shape = (
        jax.ShapeDtypeStruct((fw, seq), F32), jax.ShapeDtypeStruct((fw, seq), F32),
        jax.ShapeDtypeStruct((fh, seq), F32), jax.ShapeDtypeStruct((dw, seq), F32),
        jax.ShapeDtypeStruct((seq, dw), F32),
        jax.ShapeDtypeStruct((fh * FOX_SLOT, seq), BF16), jax.ShapeDtypeStruct((seq, fh * FOX_SLOT), BF16),
        jax.ShapeDtypeStruct((fh * VF_SLOT, seq), BF16),
        jax.ShapeDtypeStruct((ng * FOX_SLOT, seq), BF16), jax.ShapeDtypeStruct((seq, ng * FOX_SLOT), BF16),
        jax.ShapeDtypeStruct((dh * VD_SLOT, seq), BF16))
    out_specs = (col(fw), col(fw), col(fh), col(dw), row(dw),
                 col(fh * FOX_SLOT), row(fh * FOX_SLOT), col(fh * VF_SLOT),
                 col(ng * FOX_SLOT), row(ng * FOX_SLOT), col(dh * VD_SLOT))
    body = functools.partial(_proj_p_body, fw=fw, fh=fh, dw=dw, scale=HEAD_DIM ** -0.5)
    return pl.pallas_call(
        body, grid=(seq // tm,),
        in_specs=[row(d), _whole((1, d)), _whole((1, d)), _whole((n_in, d)), _whole((fh, 1)),
                  col(HEAD_DIM // 2), col(HEAD_DIM // 2)],
        out_specs=out_specs, out_shape=out_shape,
        scratch_shapes=[pltpu.VMEM((fh, LANES), F32)],
        compiler_params=_params("arbitrary"), name="proj_prompt")(x, sh, sc, w_t, b_f, cos_t, sin_t)


def _attn_p_body(qi_ref, kj_ref, qat_ref, qdat_ref, ka_ref, kda_ref, vat_ref, vdat_ref,
                 l1q, l1k, l2q, l2k, of_o, od_o, m_ref, accf_ref, accd_ref, *, fh, dh, lambda_init):
    step = pl.program_id(0)
    qi = qi_ref[step]
    kj = kj_ref[step]
    tk = ka_ref.shape[0]
    tq = qat_ref.shape[1]

    @pl.when(kj == 0)
    def _():
        m_ref[...] = jnp.full_like(m_ref, NEG)
        accf_ref[...] = jnp.zeros_like(accf_ref)
        accd_ref[...] = jnp.zeros_like(accd_ref)

    key = lax.broadcasted_iota(jnp.int32, (tk, tq), 0)
    qry = lax.broadcasted_iota(jnp.int32, (tk, tq), 1)
    visible = jnp.logical_or(kj < qi, key <= qry)

    def online_step(idx, s_t, v_t, acc_ref, a):
        s_t = jnp.where(visible, s_t, NEG)
        m_prev = m_ref[idx:idx + 1, :]
        m_new = jnp.maximum(m_prev, jnp.max(s_t, axis=0, keepdims=True))
        alpha = jnp.exp(m_prev - m_new)
        p_t = jnp.exp(s_t - m_new).astype(BF16)
        acc_ref[a] = alpha * acc_ref[a] + jnp.dot(v_t, p_t, preferred_element_type=F32)
        m_ref[idx:idx + 1, :] = m_new

    for h in range(fh):
        s_t = jnp.dot(ka_ref[:, h * FOX_SLOT:(h + 1) * FOX_SLOT], qat_ref[h * FOX_SLOT:(h + 1) * FOX_SLOT, :],
                      preferred_element_type=F32)
        online_step(h, s_t, vat_ref[h * VF_SLOT:(h + 1) * VF_SLOT, :], accf_ref, h)
    for g in range(2 * dh):
        s_t = jnp.dot(kda_ref[:, g * FOX_SLOT:(g + 1) * FOX_SLOT], qdat_ref[g * FOX_SLOT:(g + 1) * FOX_SLOT, :],
                      preferred_element_type=F32)
        online_step(fh + g, s_t, vdat_ref[(g // 2) * VD_SLOT:(g // 2 + 1) * VD_SLOT, :], accd_ref, g)

    @pl.when(kj == qi)
    def _():
        outs = []
        for h in range(fh):
            a = accf_ref[h]
            outs.append(a[0:HEAD_DIM] / a[HEAD_DIM:HEAD_DIM + 1])
        of_o[...] = jnp.concatenate(outs, axis=0).T.astype(BF16)
        lam = _lambda(l1q, l1k, l2q, l2k, lambda_init)
        e = 2 * HEAD_DIM
        outs = []
        for h in range(dh):
            a1 = accd_ref[2 * h]
            a2 = accd_ref[2 * h + 1]
            outs.append(a1[0:e] / a1[e:e + 1] - lam * (a2[0:e] / a2[e:e + 1]))
        od_o[...] = jnp.concatenate(outs, axis=0).T


def _attn_prompt(qat, qdat, ka, kda, vat, vdat, lams, *, fh, dh, lambda_init):
    seq = ka.shape[0]
    t = min(ATTN_TILE, seq)
    nb = seq // t
    pairs = [(q, k) for q in range(nb) for k in range(q + 1)]
    qi_tab = jnp.asarray([p[0] for p in pairs], jnp.int32)
    kj_tab = jnp.asarray([p[1] for p in pairs], jnp.int32)
    qcol = lambda rows: pl.BlockSpec((rows, t), lambda s, qi, kj: (0, qi[s]))
    kcol = lambda rows: pl.BlockSpec((rows, t), lambda s, qi, kj: (0, kj[s]))
    krow = lambda cols: pl.BlockSpec((t, cols), lambda s, qi, kj: (kj[s], 0))
    orow = lambda cols: pl.BlockSpec((t, cols), lambda s, qi, kj: (qi[s], 0))
    lam_spec = pl.BlockSpec((1, HEAD_DIM), lambda s, qi, kj: (0, 0))
    fw, dw = fh * HEAD_DIM, dh * 2 * HEAD_DIM
    body = functools.partial(_attn_p_body, fh=fh, dh=dh, lambda_init=lambda_init)
    return pl.pallas_call(
        body,
        grid_spec=pltpu.PrefetchScalarGridSpec(
            num_scalar_prefetch=2, grid=(len(pairs),),
            in_specs=[qcol(qat.shape[0]), qcol(qdat.shape[0]), krow(ka.shape[1]), krow(kda.shape[1]),
                      kcol(vat.shape[0]), kcol(vdat.shape[0]), lam_spec, lam_spec, lam_spec, lam_spec],
            out_specs=(orow(fw), orow(dw)),
            scratch_shapes=[pltpu.VMEM((fh + 2 * dh, t), F32), pltpu.VMEM((fh, VF_SLOT, t), F32),
                            pltpu.VMEM((2 * dh, VD_SLOT, t), F32)]),
        out_shape=(jax.ShapeDtypeStruct((seq, fw), BF16), jax.ShapeDtypeStruct((seq, dw), F32)),
        compiler_params=_params("arbitrary"), name="attn_prompt")(qi_tab, kj_tab, qat, qdat, ka, kda, vat, vdat, *lams)


def _merge_ln(x, of, od, g_a, subln_g, w_o, ln_g, ln_b, *, alpha, lambda_init):
    e = 2 * HEAD_DIM
    parts = [of.astype(BF16)]
    for h in range(od.shape[1] // e):
        seg = od[:, h * e:(h + 1) * e]
        y = seg * lax.rsqrt(jnp.mean(seg * seg, axis=-1, keepdims=True) + 1e-6) * subln_g
        parts.append((y * (1.0 - lambda_init)).astype(BF16))
    mix = jnp.dot(jnp.concatenate(parts, axis=1), w_o, preferred_element_type=F32)
    return _layer_norm(alpha * x + g_a * mix, ln_g, ln_b)


def _ffn_tail(x1, gc, u, g_f, w_down, ln_g, ln_b, *, alpha):
    act = (jax.nn.silu(gc) * u).astype(BF16)
    f = jnp.dot(act, w_down, preferred_element_type=F32)
    return _layer_norm(alpha * x1 + g_f * f, ln_g, ln_b)


def _post_p_body(x_ref, of_ref, od_ref, ga_ref, shf_ref, scf_ref, gf_ref, sg_ref, wo_ref, lag_ref, lab_ref,
                 wg_ref, wu_ref, cw_ref, cb_ref, wd_ref, lfg_ref, lfb_ref, y_o, conv_o, carry_ref,
                 *, alpha, lambda_init):
    tm = x_ref.shape[0]

    @pl.when(pl.program_id(0) == 0)
    def _():
        carry_ref[...] = jnp.zeros_like(carry_ref)

    x1 = _merge_ln(x_ref[...], of_ref[...], od_ref[...], ga_ref[...], sg_ref[...], wo_ref[...],
                   lag_ref[...], lab_ref[...], alpha=alpha, lambda_init=lambda_init)
    h2 = (x1 * (1.0 + scf_ref[...]) + shf_ref[...]).astype(BF16)
    g = jnp.dot(h2, wg_ref[...], preferred_element_type=F32)
    u = jnp.dot(h2, wu_ref[...], preferred_element_type=F32)
    ri = lax.broadcasted_iota(jnp.int32, (tm, 1), 0)
    prev = carry_ref[...]
    g1 = jnp.where(ri == 0, prev[1:2], pltpu.roll(g, 1, 0))
    g2 = jnp.where(ri == 0, prev[0:1], jnp.where(ri == 1, prev[1:2], pltpu.roll(g, 2, 0)))
    cw = cw_ref[...]
    gc = cb_ref[...] + (cw[0:1] * g2 + cw[1:2] * g1 + cw[2:3] * g)
    tail = g[tm - (CONV_WIDTH - 1):tm]
    carry_ref[...] = tail
    conv_o[...] = tail
    y_o[...] = _ffn_tail(x1, gc, u, gf_ref[...], wd_ref[...], lfg_ref[...], lfb_ref[...], alpha=alpha)


def _post_s_body(x_ref, of_ref, od_ref, ga_ref, shf_ref, scf_ref, gf_ref, p0_ref, p1_ref, sg_ref, wo_ref,
                 lag_ref, lab_ref, wg_ref, wu_ref, cw_ref, cb_ref, wd_ref, lfg_ref, lfb_ref, y_o, g_o,
                 *, alpha, lambda_init):
    x1 = _merge_ln(x_ref[...], of_ref[...], od_ref[...], ga_ref[...], sg_ref[...], wo_ref[...],
                   lag_ref[...], lab_ref[...], alpha=alpha, lambda_init=lambda_init)
    h2 = (x1 * (1.0 + scf_ref[...]) + shf_ref[...]).astype(BF16)
    g = jnp.dot(h2, wg_ref[...], preferred_element_type=F32)
    u = jnp.dot(h2, wu_ref[...], preferred_element_type=F32)
    cw = cw_ref[...]
    gc = cb_ref[...] + (cw[0:1] * p0_ref[...] + cw[1:2] * p1_ref[...] + cw[2:3] * g)
    g_o[...] = g
    y_o[...] = _ffn_tail(x1, gc, u, gf_ref[...], wd_ref[...], lfg_ref[...], lfb_ref[...], alpha=alpha)


def _weight_specs(d, dff):
    e = 2 * HEAD_DIM
    return [_whole((1, e)), _whole((d, d)), _whole((1, d)), _whole((1, d)), _whole((d, dff)), _whole((d, dff)),
            _whole((CONV_WIDTH, dff)), _whole((1, dff)), _whole((dff, d)), _whole((1, d)), _whole((1, d))]


def _post_prompt(x, of, od, mods, weights, *, alpha, lambda_init):
    seq, d = x.shape
    dff = weights[4].shape[1]
    tm = min(POST_TILE, seq)
    row = lambda cols: pl.BlockSpec((tm, cols), lambda i: (i, 0))
    body = functools.partial(_post_p_body, alpha=alpha, lambda_init=lambda_init)
    return pl.pallas_call(
        body, grid=(seq // tm,),
        in_specs=[row(d), row(of.shape[1]), row(od.shape[1])] + [_whole((1, d))] * 4 + _weight_specs(d, dff),
        out_specs=(row(d), _whole_out((CONV_WIDTH - 1, dff))),
        out_shape=(jax.ShapeDtypeStruct((seq, d), F32), jax.ShapeDtypeStruct((CONV_WIDTH - 1, dff), F32)),
        scratch_shapes=[pltpu.VMEM((CONV_WIDTH - 1, dff), F32)],
        compiler_params=_params("arbitrary"), name="post_prompt")(x, of, od, *mods, *weights)


def _post_sample(x, of, od, mods, prev0, prev1, weights, *, alpha, lambda_init):
    b, d = x.shape
    dff = weights[4].shape[1]
    body = functools.partial(_post_s_body, alpha=alpha, lambda_init=lambda_init)
    return pl.pallas_call(
        body, grid=(1,),
        in_specs=[_whole((b, d)), _whole(of.shape), _whole(od.shape)] + [_whole((b, d))] * 4
        + [_whole((b, dff))] * 2 + _weight_specs(d, dff),
        out_specs=(_whole_out((b, d)), _whole_out((b, dff))),
        out_shape=(jax.ShapeDtypeStruct((b, d), F32), jax.ShapeDtypeStruct((b, dff), F32)),
        compiler_params=_params("arbitrary"), name="post_sample")(x, of, od, *mods, prev0, prev1, *weights)


def _rotary_n(x, cos, sin):
    half = HEAD_DIM // 2
    out = []
    for g in range(x.shape[1] // HEAD_DIM):
        x1 = x[:, g * HEAD_DIM: g * HEAD_DIM + half]
        x2 = x[:, g * HEAD_DIM + half: (g + 1) * HEAD_DIM]
        out += [x1 * cos - x2 * sin, x1 * sin + x2 * cos]
    return jnp.concatenate(out, axis=1)


def _proj_s_body(x_ref, sh_ref, sc_ref, w_ref, bf_ref, cos_ref, sin_ref,
                 qf_o, kf_o, vf_o, lf_o, qd_o, kd_o, vd_o, *, fw, fh, dw, scale):
    hb = (x_ref[...] * (1.0 + sc_ref[...]) + sh_ref[...]).astype(BF16)
    z = jnp.dot(hb, w_ref[...], preferred_element_type=F32)
    o_qd = 3 * fw
    o_fl = o_qd + 3 * dw
    qf_o[...] = z[:, 0:fw] * scale
    kf_o[...] = z[:, fw:2 * fw]
    vf_o[...] = z[:, 2 * fw:3 * fw]
    lf_o[...] = _log_sigmoid(z[:, o_fl:o_fl + fh] + bf_ref[...])
    qd_o[...] = _rotary_n(z[:, o_qd:o_qd + dw], cos_ref[...], sin_ref[...]) * scale
    kd_o[...] = _rotary_n(z[:, o_qd + dw:o_qd + 2 * dw], cos_ref[...], sin_ref[...])
    vd_o[...] = z[:, o_qd + 2 * dw:o_qd + 3 * dw]


def _proj_sample(x, sh, sc, w_n, b_f, cos, sin, *, fw, fh, dw):
    b, d = x.shape
    shapes = [(b, fw), (b, fw), (b, fw), (b, fh), (b, dw), (b, dw), (b, dw)]
    body = functools.partial(_proj_s_body, fw=fw, fh=fh, dw=dw, scale=HEAD_DIM ** -0.5)
    return pl.pallas_call(
        body, grid=(1,),
        in_specs=[_whole((b, d)), _whole((b, d)), _whole((b, d)), _whole(w_n.shape), _whole((1, fh)),
                  _whole((1, HEAD_DIM // 2)), _whole((1, HEAD_DIM // 2))],
        out_specs=tuple(_whole_out(s) for s in shapes),
        out_shape=tuple(jax.ShapeDtypeStruct(s, F32) for s in shapes),
        compiler_params=_params("arbitrary"), name="proj_sample")(x, sh, sc, w_n, b_f, cos, sin)


def _decode_body(pt_ref, qf_ref, kfs_ref, vfs_ref, lfs_ref, qd_ref, kds_ref, vds_ref, l1q, l1k, l2q, l2k,
                 *rest, n_pages_step, dh, lambda_init):
    g_pages = n_pages_step
    pages = rest[:5 * g_pages]
    of_o, od_o = rest[5 * g_pages:5 * g_pages + 2]
    carry, mf, lf, accf, md, ld, accd = rest[5 * g_pages + 2:]
    c = pl.program_id(1)
    groups, fw = accf.shape
    page = pages[0].shape[2]
    e = 2 * HEAD_DIM

    gi = lax.broadcasted_iota(jnp.int32, (groups, fw), 0)
    li = lax.broadcasted_iota(jnp.int32, (groups, fw), 1)
    own = (li // HEAD_DIM) == gi
    q_f = jnp.where(own, qf_ref[0], 0.0).astype(BF16)
    q_d = jnp.where(own, qd_ref[0], 0.0).astype(BF16)

    @pl.when(c == 0)
    def _():
        carry[...] = jnp.broadcast_to(lfs_ref[0], carry.shape)
        mf[...] = jnp.full_like(mf, NEG)
        md[...] = jnp.full_like(md, NEG)
        lf[...] = jnp.zeros_like(lf)
        ld[...] = jnp.zeros_like(ld)
        accf[...] = jnp.zeros_like(accf)
        accd[...] = jnp.zeros_like(accd)

    jj = lax.broadcasted_iota(jnp.int32, (page, page), 0)
    tt = lax.broadcasted_iota(jnp.int32, (page, page), 1)
    later = (jj > tt).astype(BF16)
    et = lax.broadcasted_iota(jnp.int32, (page, dh * page), 0)
    er = lax.broadcasted_iota(jnp.int32, (page, dh * page), 1)
    spread = ((er // dh) == et).astype(BF16)
    hr = lax.broadcasted_iota(jnp.int32, (groups, dh * page), 1)
    hg = lax.broadcasted_iota(jnp.int32, (groups, dh * page), 0)
    own_head = (hr % dh) == (hg // 2)

    def online(m_ref, l_ref, s):
        m_prev = m_ref[:, 0:1]
        m_new = jnp.maximum(m_prev, jnp.max(s, axis=1, keepdims=True))
        alpha = jnp.exp(m_prev - m_new)
        p = jnp.exp(s - m_new)
        l_ref[...] = jnp.broadcast_to(alpha * l_ref[:, 0:1] + jnp.sum(p, axis=1, keepdims=True), l_ref.shape)
        m_ref[...] = jnp.broadcast_to(m_new, m_ref.shape)
        return alpha, p

    for i in reversed(range(g_pages)):
        fk, fv, lfp, dk, dv = (pages[k * g_pages + i] for k in range(5))
        s = jnp.dot(q_f, fk[0].astype(BF16), preferred_element_type=F32)
        lf_t = lfp[0]
        after = carry[:, 0:1]
        s = s + (after + _dot3(_split3(lf_t), later))
        carry[...] = jnp.broadcast_to(after + jnp.sum(lf_t, axis=1, keepdims=True), carry.shape)
        alpha, p = online(mf, lf, s)
        pv = lax.dot_general(p.astype(BF16), fv[0].astype(BF16), _NT, preferred_element_type=F32)
        accf[...] = alpha * accf[...] + pv

        s = jnp.dot(q_d, dk[0].astype(BF16), preferred_element_type=F32)
        alpha, p = online(md, ld, s)
        pe = jnp.dot(p.astype(BF16), spread, preferred_element_type=F32)
        pe = jnp.where(own_head, pe, 0.0).astype(BF16)
        accd[...] = alpha * accd[...] + jnp.dot(pe, dv[0].astype(BF16), preferred_element_type=F32)

    @pl.when(c == pl.num_programs(1) - 1)
    def _():
        rnd = lambda v: v.astype(BF16).astype(F32)
        s = jnp.sum(q_f.astype(F32) * rnd(kfs_ref[0]), axis=1, keepdims=True)
        alpha, p = online(mf, lf, s)
        acc = alpha * accf[...] + rnd(p) * rnd(vfs_ref[0])
        of_o[0] = jnp.sum(jnp.where(own, acc / lf[:, 0:1], 0.0), axis=0, keepdims=True)

        s = jnp.sum(q_d.astype(F32) * rnd(kds_ref[0]), axis=1, keepdims=True)
        alpha, p = online(md, ld, s)
        o = (alpha * accd[...] + rnd(p) * rnd(vds_ref[0])) / ld[:, 0:1]
        lam = _lambda(l1q, l1k, l2q, l2k, lambda_init)
        od_o[0] = jnp.concatenate([o[2 * h:2 * h + 1] - lam * o[2 * h + 1:2 * h + 2] for h in range(dh)], axis=0)


def _decode(page_table, qf, kfs, vfs, lfs, qd, kds, vds, lams, fk, fv, lfp, dk, dv, *, dh, lambda_init):
    b, n_pages = page_table.shape
    fw = qf.shape[-1]
    page = fk.shape[2]
    gp = math.gcd(DECODE_PAGES, n_pages)
    nch = n_pages // gp
    groups = fw // HEAD_DIM
    e = 2 * HEAD_DIM
    per_sample = lambda arr: pl.BlockSpec((1,) + arr.shape[1:], lambda s, c, pt: (s, 0, 0))
    lam_spec = pl.BlockSpec((1, HEAD_DIM), lambda s, c, pt: (0, 0))

    def page_spec(arr, i):
        return pl.BlockSpec((1,) + arr.shape[1:],
                            lambda s, c, pt, i=i: (pt[s * n_pages + (nch - 1 - c) * gp + i], 0, 0))

    page_ops, page_specs = [], []
    for arr in (fk, fv, lfp, dk, dv):
        for i in range(gp):
            page_ops.append(arr)
            page_specs.append(page_spec(arr, i))
    small = (qf, kfs, vfs, lfs, qd, kds, vds)
    body = functools.partial(_decode_body, n_pages_step=gp, dh=dh, lambda_init=lambda_init)
    stat = pltpu.VMEM((groups, LANES), F32)
    return pl.pallas_call(
        body,
        grid_spec=pltpu.PrefetchScalarGridSpec(
            num_scalar_prefetch=1, grid=(b, nch),
            in_specs=[per_sample(a) for a in small] + [lam_spec] * 4 + page_specs,
            out_specs=(pl.BlockSpec((1, 1, fw), lambda s, c, pt: (s, 0, 0)),
                       pl.BlockSpec((1, dh, e), lambda s, c, pt: (s, 0, 0))),
            scratch_shapes=[stat, stat, stat, pltpu.VMEM((groups, fw), F32), stat, stat,
                            pltpu.VMEM((groups, e), F32)]),
        out_shape=(jax.ShapeDtypeStruct((b, 1, fw), F32), jax.ShapeDtypeStruct((b, dh, e), F32)),
        compiler_params=_params("arbitrary", "arbitrary"), name="decode")(
            page_table.reshape(-1), *small, *lams, *page_ops)


def _rope_tables(pos):
    half = HEAD_DIM // 2
    inv = ROPE_THETA ** (-jnp.arange(half, dtype=F32) / half)
    ang = pos.astype(F32)[:, None] * inv[None, :]
    return jnp.cos(ang), jnp.sin(ang)


def kernel(x_prompt, x_sample, cache_fox_k, cache_fox_v, cache_fox_logf, cache_diff_k, cache_diff_v, state_conv, page_table, c_prompt, c_sample, w_ada, b_ada, w_in, b_fgate, lambda_q1, lambda_k1, lambda_q2, lambda_k2, subln_g, w_o, ln_a_g, ln_a_b, w_gate, w_up, conv_w, conv_b, w_down, ln_f_g, ln_f_b):
    batch, seq, d = x_prompt.shape
    b_dec, dec_seq, _ = x_sample.shape
    depth, n_pool, page, fh, _ = cache_fox_k.shape
    dh = cache_diff_k.shape[3]
    fw, dw = fh * HEAD_DIM, dh * 2 * HEAD_DIM
    past_len = page_table.shape[1] * page
    dff = w_gate.shape[-1]
    assert batch == 1 and dec_seq == 1 and fw + dw == d and w_in.shape[-1] == 3 * fw + fh + 3 * dw
    assert math.log2(HEAD_DIM).is_integer()
    alpha = (2 * depth) ** 0.25

    xp = x_prompt.reshape(seq, d)
    xs = x_sample.reshape(b_dec, d)
    c_all = jnp.concatenate([c_prompt, c_sample], axis=0)
    pad = (-c_all.shape[0]) % SUBLANES
    c_all = jnp.pad(c_all, ((0, pad), (0, 0)))
    cos_p, sin_p = _rope_tables(jnp.arange(seq))
    cos_s, sin_s = _rope_tables(past_len + jnp.arange(dec_seq))
    row2 = lambda v: v.reshape(1, -1)

    new_p = [[] for _ in range(6)]
    new_s = [[] for _ in range(6)]
    for l in range(depth):
        lambda_init = 0.8 - 0.6 * math.exp(-0.3 * l)
        mod = _adaln(c_all, w_ada[l], row2(b_ada[l]))
        mod_p = [mod[0:1, k * d:(k + 1) * d] for k in range(6)]
        mod_s = [mod[1:1 + b_dec, k * d:(k + 1) * d] for k in range(6)]
        lams = tuple(row2(v[l]) for v in (lambda_q1, lambda_k1, lambda_q2, lambda_k2))
        weights = (row2(subln_g[l]), w_o[l].astype(BF16), row2(ln_a_g[l]), row2(ln_a_b[l]),
                   w_gate[l].astype(BF16), w_up[l].astype(BF16), conv_w[l], row2(conv_b[l]),
                   w_down[l].astype(BF16), row2(ln_f_g[l]), row2(ln_f_b[l]))

        w_t = w_in[l].T.astype(BF16)
        (kf_t, vf_t, lf_t, kd_t, vd, qat, ka, vat, qdat, kda, vdat) = _proj_prompt(
            xp, mod_p[0], mod_p[1], w_t, b_fgate[l].reshape(fh, 1), cos_p.T, sin_p.T, fw=fw, fh=fh, dw=dw)
        of, od = _attn_prompt(qat, qdat, ka, kda, vat, vdat, lams, fh=fh, dh=dh, lambda_init=lambda_init)
        xp, conv_new_p = _post_prompt(xp, of, od, mod_p[2:6], weights, alpha=alpha, lambda_init=lambda_init)
        new_p[0].append(jnp.transpose(kf_t.reshape(1, fh, HEAD_DIM, seq), (0, 3, 1, 2)))
        new_p[1].append(jnp.transpose(vf_t.reshape(1, fh, HEAD_DIM, seq), (0, 3, 1, 2)))
        new_p[2].append(jnp.transpose(lf_t.reshape(1, fh, seq), (0, 2, 1)))
        new_p[3].append(jnp.transpose(kd_t.reshape(1, dh, 2, HEAD_DIM, seq), (0, 4, 1, 2, 3)))
        new_p[4].append(vd.reshape(1, seq, dh, 2 * HEAD_DIM))
        new_p[5].append(conv_new_p.reshape(1, CONV_WIDTH - 1, dff))

        wl = w_in[l]
        w_n = jnp.concatenate([wl[:, :3 * fw], wl[:, 3 * fw + fh:], wl[:, 3 * fw:3 * fw + fh]], axis=1).astype(BF16)
        qf, kf, vf, lf, qd, kd, vd_s = _proj_sample(
            xs, mod_s[0], mod_s[1], w_n, row2(b_fgate[l]), cos_s, sin_s, fw=fw, fh=fh, dw=dw)
        fk_v = jnp.transpose(cache_fox_k[l], (0, 2, 3, 1)).reshape(n_pool, fw, page)
        fv_v = jnp.transpose(cache_fox_v[l], (0, 2, 3, 1)).reshape(n_pool, fw, page)
        lf_v = jnp.transpose(cache_fox_logf[l], (0, 2, 1))
        dk_v = jnp.transpose(cache_diff_k[l], (0, 2, 3, 4, 1)).reshape(n_pool, dw, page)
        dv_v = cache_diff_v[l].reshape(n_pool, page * dh, 2 * HEAD_DIM)
        per = lambda a: a.reshape(b_dec, 1, a.shape[-1])
        vds = jnp.repeat(vd_s.reshape(b_dec, dh, 2 * HEAD_DIM), 2, axis=1)
        of_s, od_s = _decode(page_table, per(qf), per(kf), per(vf), lf.reshape(b_dec, fh, 1), per(qd), per(kd), vds,
                             lams, fk_v, fv_v, lf_v, dk_v, dv_v, dh=dh, lambda_init=lambda_init)
        st = state_conv[l]
        xs, g_s = _post_sample(xs, of_s.reshape(b_dec, fw), od_s.reshape(b_dec, dw), mod_s[2:6],
                               st[:, 0], st[:, 1], weights, alpha=alpha, lambda_init=lambda_init)
        new_s[0].append(kf.reshape(b_dec, 1, fh, HEAD_DIM))
        new_s[1].append(vf.reshape(b_dec, 1, fh, HEAD_DIM))
        new_s[2].append(lf.reshape(b_dec, 1, fh))
        new_s[3].append(kd.reshape(b_dec, 1, dh, 2, HEAD_DIM))
        new_s[4].append(vd_s.reshape(b_dec, 1, dh, 2 * HEAD_DIM))
        new_s[5].append(jnp.stack([st[:, 1], g_s], axis=1))

    outs_p = [jnp.stack(a) for a in new_p]
    outs_s = [jnp.stack(a) for a in new_s]
    return (xp.reshape(batch, seq, d), xs.reshape(b_dec, dec_seq, d), *outs_p, *outs_s)
```

```python
import functools
import math

import jax
import jax.numpy as jnp
from jax import lax
from jax.experimental import pallas as pl
from jax.experimental.pallas import tpu as pltpu

F32, BF16 = jnp.float32, jnp.bfloat16

HEAD_DIM = 64
ROPE_THETA = 10000.0
CONV_WIDTH = 3
NEG = -1e30
LOG2E = math.log2(math.e)
LANES = 128
SUBLANES = 8
VMEM_LIMIT_BYTES = 56 * 1024 * 1024

PROJ_TILE = 512
ATTN_TILE = 512
ATTN_UNIT = 512
POST_TILE = 256
ADALN_TILE = 1024
DECODE_PAGES = 16

FOX_SLOT = 128
VF_SLOT = HEAD_DIM + SUBLANES
VD_SLOT = 2 * HEAD_DIM + SUBLANES

_NT = (((1,), (1,)), ((), ()))


def _params(*sem):
    return pltpu.CompilerParams(dimension_semantics=sem, vmem_limit_bytes=VMEM_LIMIT_BYTES)


def _whole(shape):
    return pl.BlockSpec(shape, lambda *_: (0,) * len(shape), pipeline_mode=pl.Buffered(1))


def _whole_out(shape):
    return pl.BlockSpec(shape, lambda *_: (0,) * len(shape))


def _log_sigmoid(x):
    return jnp.minimum(x, 0.0) - jnp.log1p(jnp.exp(-jnp.abs(x)))


def _split3(x):
    hi = x.astype(BF16)
    r = x - hi.astype(F32)
    mid = r.astype(BF16)
    lo = (r - mid.astype(F32)).astype(BF16)
    return hi, mid, lo


def _dot3(parts, rhs):
    return sum(jnp.dot(p, rhs, preferred_element_type=F32) for p in parts)


def _layer_norm(x, g, b, eps=1e-5):
    mu = jnp.mean(x, axis=-1, keepdims=True)
    xc = x - mu
    var = jnp.mean(xc * xc, axis=-1, keepdims=True)
    return xc * lax.rsqrt(var + eps) * g + b


def _lambda(l1q, l1k, l2q, l2k, lambda_init):
    s1 = jnp.sum(l1q[...] * l1k[...], axis=-1, keepdims=True)
    s2 = jnp.sum(l2q[...] * l2k[...], axis=-1, keepdims=True)
    return jnp.exp(s1) - jnp.exp(s2) + lambda_init


def _adaln_body(c_ref, w_ref, b_ref, o_ref):
    a = jax.nn.silu(c_ref[...]).astype(BF16)
    o_ref[...] = jnp.dot(a, w_ref[...].astype(BF16), preferred_element_type=F32) + b_ref[...]


def _adaln(c_all, w, b):
    rows, d = c_all.shape
    n = w.shape[1]
    tn = min(ADALN_TILE, n)
    return pl.pallas_call(
        _adaln_body, grid=(n // tn,),
        in_specs=[_whole((rows, d)), pl.BlockSpec((d, tn), lambda j: (0, j)), pl.BlockSpec((1, tn), lambda j: (0, j))],
        out_specs=pl.BlockSpec((rows, tn), lambda j: (0, j)),
        out_shape=jax.ShapeDtypeStruct((rows, n), F32),
        compiler_params=_params("arbitrary"), name="adaln")(c_all, w, b)


def _rotary_t(x_t, cos_t, sin_t):
    half = HEAD_DIM // 2
    out = []
    for g in range(x_t.shape[0] // HEAD_DIM):
        x1 = x_t[g * HEAD_DIM: g * HEAD_DIM + half]
        x2 = x_t[g * HEAD_DIM + half: (g + 1) * HEAD_DIM]
        out += [x1 * cos_t - x2 * sin_t, x1 * sin_t + x2 * cos_t]
    return jnp.concatenate(out, axis=0)


def _proj_p_body(x_ref, sh_ref, sc_ref, wt_ref, bf_ref, cos_ref, sin_ref,
                 kft_o, vft_o, lft_o, kdt_o, vd_o, qat_o, ka_o, vat_o, qdat_o, kda_o, vdat_o,
                 carry_ref, *, fw, fh, dw, scale):
    tm = x_ref.shape[0]

    @pl.when(pl.program_id(0) == 0)
    def _():
        carry_ref[...] = jnp.zeros_like(carry_ref)

    hb = (x_ref[...] * (1.0 + sc_ref[...]) + sh_ref[...]).astype(BF16)
    z_t = lax.dot_general(wt_ref[...], hb, _NT, preferred_element_type=F32)
    o_fl = 3 * fw
    o_qd = o_fl + fh
    qf_t = z_t[0:fw] * scale
    kf_t = z_t[fw:2 * fw]
    vf_t = z_t[2 * fw:3 * fw]
    lf_t = _log_sigmoid(z_t[o_fl:o_fl + fh] + bf_ref[...])
    qd_t = _rotary_t(z_t[o_qd:o_qd + dw], cos_ref[...], sin_ref[...]) * scale
    kd_t = _rotary_t(z_t[o_qd + dw:o_qd + 2 * dw], cos_ref[...], sin_ref[...])
    vd_t = z_t[o_qd + 2 * dw:o_qd + 3 * dw]

    j = lax.broadcasted_iota(jnp.int32, (tm, tm), 0)
    t = lax.broadcasted_iota(jnp.int32, (tm, tm), 1)
    c_t = _dot3(_split3(lf_t), (j <= t).astype(BF16)) + carry_ref[:, 0:1]
    carry_ref[...] = jnp.broadcast_to(c_t[:, tm - 1:tm], carry_ref.shape)

    kft_o[...] = kf_t
    vft_o[...] = vf_t
    lft_o[...] = lf_t
    kdt_o[...] = kd_t
    vd_o[...] = vd_t.T

    c_t = c_t * LOG2E
    chi, cmid, clo = (p.astype(F32) for p in _split3(c_t))
    tail_rows = FOX_SLOT - HEAD_DIM
    ri = lax.broadcasted_iota(jnp.int32, (tail_rows, tm), 0)
    qa = []
    for h in range(fh):
        tail = jnp.where(ri == 0, chi[h:h + 1], jnp.where(ri == 1, cmid[h:h + 1], jnp.where(
            ri == 2, clo[h:h + 1], jnp.where(ri < 6, 1.0, 0.0))))
        qa += [qf_t[h * HEAD_DIM:(h + 1) * HEAD_DIM], tail]
    qat_o[...] = jnp.concatenate(qa, axis=0).astype(BF16)

    kf = kf_t.T
    c_n = c_t.T
    nhi, nmid, nlo = (p.astype(F32) for p in _split3(c_n))
    li = lax.broadcasted_iota(jnp.int32, (tm, tail_rows), 1)
    ka = []
    for h in range(fh):
        tail = jnp.where(li < 3, 1.0, jnp.where(li == 3, -nhi[:, h:h + 1], jnp.where(
            li == 4, -nmid[:, h:h + 1], jnp.where(li == 5, -nlo[:, h:h + 1], 0.0))))
        ka += [kf[:, h * HEAD_DIM:(h + 1) * HEAD_DIM], tail]
    ka_o[...] = jnp.concatenate(ka, axis=1).astype(BF16)

    ones_row = (lax.broadcasted_iota(jnp.int32, (SUBLANES, tm), 0) == 0).astype(F32)
    va = []
    for h in range(fh):
        va += [vf_t[h * HEAD_DIM:(h + 1) * HEAD_DIM], ones_row]
    vat_o[...] = jnp.concatenate(va, axis=0).astype(BF16)

    zq = jnp.zeros((FOX_SLOT - HEAD_DIM, tm), F32)
    qda = []
    for g in range(dw // HEAD_DIM):
        qda += [qd_t[g * HEAD_DIM:(g + 1) * HEAD_DIM], zq]
    qdat_o[...] = jnp.concatenate(qda, axis=0).astype(BF16)

    kd = kd_t.T
    zk = jnp.zeros((tm, FOX_SLOT - HEAD_DIM), F32)
    kda = []
    for g in range(dw // HEAD_DIM):
        kda += [kd[:, g * HEAD_DIM:(g + 1) * HEAD_DIM], zk]
    kda_o[...] = jnp.concatenate(kda, axis=1).astype(BF16)

    vda = []
    for h in range(dw // (2 * HEAD_DIM)):
        vda += [vd_t[h * 2 * HEAD_DIM:(h + 1) * 2 * HEAD_DIM], ones_row]
    vdat_o[...] = jnp.concatenate(vda, axis=0).astype(BF16)


def _proj_prompt(x, sh, sc, w_t, b_f, cos_t, sin_t, *, fw, fh, dw):
    seq, d = x.shape
    n_in = w_t.shape[0]
    tm = min(PROJ_TILE, seq)
    dh = dw // (2 * HEAD_DIM)
    ng = dw // HEAD_DIM
    col = lambda rows: pl.BlockSpec((rows, tm), lambda i: (0, i))
    row = lambda cols: pl.BlockSpec((tm, cols), lambda i: (i, 0))
    out_shape = (
        jax.ShapeDtypeStruct((fw, seq), F32), jax.ShapeDtypeStruct((fw, seq), F32),
        jax.ShapeDtypeStruct((fh, seq), F32), jax.ShapeDtypeStruct((dw, seq), F32),
        jax.ShapeDtypeStruct((seq, dw), F32),
        jax.ShapeDtypeStruct((fh * FOX_SLOT, seq), BF16), jax.ShapeDtypeStruct((seq, fh * FOX_SLOT), BF16),
        jax.ShapeDtypeStruct((fh * VF_SLOT, seq), BF16),
        jax.ShapeDtypeStruct((ng * FOX_SLOT, seq), BF16), jax.ShapeDtypeStruct((seq, ng * FOX_SLOT), BF16),
        jax.ShapeDtypeStruct((dh * VD_SLOT, seq), BF16))
    out_specs = (col(fw), col(fw), col(fh), col(dw), row(dw),
                 col(fh * FOX_SLOT), row(fh * FOX_SLOT), col(fh * VF_SLOT),
                 col(ng * FOX_SLOT), row(ng * FOX_SLOT), col(dh * VD_SLOT))
    body = functools.partial(_proj_p_body, fw=fw, fh=fh, dw=dw, scale=HEAD_DIM ** -0.5 * LOG2E)
    return pl.pallas_call(
        body, grid=(seq // tm,),
        in_specs=[row(d), _whole((1, d)), _whole((1, d)), _whole((n_in, d)), _whole((fh, 1)),
                  col(HEAD_DIM // 2), col(HEAD_DIM // 2)],
        out_specs=out_specs, out_shape=out_shape,
        scratch_shapes=[pltpu.VMEM((fh, LANES), F32)],
        compiler_params=_params("arbitrary"), name="proj_prompt")(x, sh, sc, w_t, b_f, cos_t, sin_t)


def _attn_p_body(qi_ref, kj_ref, qat_ref, qdat_ref, ka_ref, kda_ref, vat_ref, vdat_ref,
                 l1q, l1k, l2q, l2k, of_o, od_o, m_ref, accf_ref, accd_ref, *, fh, dh, lambda_init):
    step = pl.program_id(0)
    qi = qi_ref[step]
    kj = kj_ref[step]
    tk = ka_ref.shape[0]
    tq = qat_ref.shape[1]
    tqs = min(ATTN_UNIT, tq)

    @pl.when(kj == 0)
    def _():
        m_ref[...] = jnp.full_like(m_ref, NEG)
        accf_ref[...] = jnp.zeros_like(accf_ref)
        accd_ref[...] = jnp.zeros_like(accd_ref)

    units = [(idx, sub) for idx in range(fh + 2 * dh) for sub in range(tq // tqs)]

    def operands(idx):
        if idx < fh:
            return (ka_ref, qat_ref, idx * FOX_SLOT, vat_ref, idx * VF_SLOT, VF_SLOT, accf_ref, idx)
        g = idx - fh
        return (kda_ref, qdat_ref, g * FOX_SLOT, vdat_ref, (g // 2) * VD_SLOT, VD_SLOT, accd_ref, g)

    def scores(n, masked):
        idx, sub = units[n]
        k_ref, q_ref, o, _, _, _, _, _ = operands(idx)
        cols = slice(sub * tqs, (sub + 1) * tqs)
        s_t = jnp.dot(k_ref[:, o:o + FOX_SLOT], q_ref[o:o + FOX_SLOT, cols], preferred_element_type=F32)
        if masked:
            key = lax.broadcasted_iota(jnp.int32, (tk, tqs), 0)
            qry = lax.broadcasted_iota(jnp.int32, (tk, tqs), 1) + sub * tqs
            s_t = jnp.where(key <= qry, s_t, NEG)
        return s_t

    def probs(n, s_t):
        idx, sub = units[n]
        cols = slice(sub * tqs, (sub + 1) * tqs)
        m_prev = m_ref[idx:idx + 1, cols]
        m_new = jnp.maximum(m_prev, jnp.max(s_t, axis=0, keepdims=True))
        return jnp.exp2(s_t - m_new).astype(BF16), m_prev, m_new

    def accumulate(n, p_t, m_prev, m_new):
        idx, sub = units[n]
        _, _, _, v_ref, vo, vs, acc_ref, a = operands(idx)
        cols = slice(sub * tqs, (sub + 1) * tqs)
        pv = jnp.dot(v_ref[vo:vo + vs, :], p_t, preferred_element_type=F32)
        acc_ref[a, :, cols] = jnp.exp2(m_prev - m_new) * acc_ref[a, :, cols] + pv
        m_ref[idx:idx + 1, cols] = m_new

    def run(masked):
        s_vals, p_vals = {}, {}
        for n in range(len(units) + 2):
            if n < len(units):
                s_vals[n] = scores(n, masked)
            if 0 <= n - 1 < len(units):
                p_vals[n - 1] = probs(n - 1, s_vals.pop(n - 1))
            if 0 <= n - 2 < len(units):
                accumulate(n - 2, *p_vals.pop(n - 2))

    @pl.when(kj < qi)
    def _():
        run(masked=False)

    @pl.when(kj == qi)
    def _():
        run(masked=True)
        outs = []
        for h in range(fh):
            a = accf_ref[h]
            outs.append(a[0:HEAD_DIM] / a[HEAD_DIM:HEAD_DIM + 1])
        of_o[...] = jnp.concatenate(outs, axis=0).T.astype(BF16)
        lam = _lambda(l1q, l1k, l2q, l2k, lambda_init)
        e = 2 * HEAD_DIM
        outs = []
        for h in range(dh):
            a1 = accd_ref[2 * h]
            a2 = accd_ref[2 * h + 1]
            outs.append(a1[0:e] / a1[e:e + 1] - lam * (a2[0:e] / a2[e:e + 1]))
        od_o[...] = jnp.concatenate(outs, axis=0).T


def _attn_prompt(qat, qdat, ka, kda, vat, vdat, lams, *, fh, dh, lambda_init):
    seq = ka.shape[0]
    t = min(ATTN_TILE, seq)
    nb = seq // t
    pairs = [(q, k) for q in range(nb) for k in range(q + 1)]
    qi_tab = jnp.asarray([p[0] for p in pairs], jnp.int32)
    kj_tab = jnp.asarray([p[1] for p in pairs], jnp.int32)
    qcol = lambda rows: pl.BlockSpec((rows, t), lambda s, qi, kj: (0, qi[s]))
    kcol = lambda rows: pl.BlockSpec((rows, t), lambda s, qi, kj: (0, kj[s]))
    krow = lambda cols: pl.BlockSpec((t, cols), lambda s, qi, kj: (kj[s], 0))
    orow = lambda cols: pl.BlockSpec((t, cols), lambda s, qi, kj: (qi[s], 0))
    lam_spec = pl.BlockSpec((1, HEAD_DIM), lambda s, qi, kj: (0, 0))
    fw, dw = fh * HEAD_DIM, dh * 2 * HEAD_DIM
    body = functools.partial(_attn_p_body, fh=fh, dh=dh, lambda_init=lambda_init)
    return pl.pallas_call(
        body,
        grid_spec=pltpu.PrefetchScalarGridSpec(
            num_scalar_prefetch=2, grid=(len(pairs),),
            in_specs=[qcol(qat.shape[0]), qcol(qdat.shape[0]), krow(ka.shape[1]), krow(kda.shape[1]),
                      kcol(vat.shape[0]), kcol(vdat.shape[0]), lam_spec, lam_spec, lam_spec, lam_spec],
            out_specs=(orow(fw), orow(dw)),
            scratch_shapes=[pltpu.VMEM((fh + 2 * dh, t), F32), pltpu.VMEM((fh, VF_SLOT, t), F32),
                            pltpu.VMEM((2 * dh, VD_SLOT, t), F32)]),
        out_shape=(jax.ShapeDtypeStruct((seq, fw), BF16), jax.ShapeDtypeStruct((seq, dw), F32)),
        compiler_params=_params("arbitrary"), name="attn_prompt")(qi_tab, kj_tab, qat, qdat, ka, kda, vat, vdat, *lams)


def _merge_ln(x, of, od, g_a, subln_g, w_o, ln_g, ln_b, *, alpha, lambda_init):
    e = 2 * HEAD_DIM
    parts = [of.astype(BF16)]
    for h in range(od.shape[1] // e):
        seg = od[:, h * e:(h + 1) * e]
        y = seg * lax.rsqrt(jnp.mean(seg * seg, axis=-1, keepdims=True) + 1e-6) * subln_g
        parts.append((y * (1.0 - lambda_init)).astype(BF16))
    mix = jnp.dot(jnp.concatenate(parts, axis=1), w_o, preferred_element_type=F32)
    return _layer_norm(alpha * x + g_a * mix, ln_g, ln_b)


def _ffn_tail(x1, gc, u, g_f, w_down, ln_g, ln_b, *, alpha):
    act = (jax.nn.silu(gc) * u).astype(BF16)
    f = jnp.dot(act, w_down, preferred_element_type=F32)
    return _layer_norm(alpha * x1 + g_f * f, ln_g, ln_b)


def _post_p_body(x_ref, of_ref, od_ref, ga_ref, shf_ref, scf_ref, gf_ref, sg_ref, wo_ref, lag_ref, lab_ref,
                 wg_ref, wu_ref, cw_ref, cb_ref, wd_ref, lfg_ref, lfb_ref, y_o, conv_o, carry_ref,
                 *, alpha, lambda_init):
    tm = x_ref.shape[0]

    @pl.when(pl.program_id(0) == 0)
    def _():
        carry_ref[...] = jnp.zeros_like(carry_ref)

    x1 = _merge_ln(x_ref[...], of_ref[...], od_ref[...], ga_ref[...], sg_ref[...], wo_ref[...],
                   lag_ref[...], lab_ref[...], alpha=alpha, lambda_init=lambda_init)
    h2 = (x1 * (1.0 + scf_ref[...]) + shf_ref[...]).astype(BF16)
    g = jnp.dot(h2, wg_ref[...], preferred_element_type=F32)
    u = jnp.dot(h2, wu_ref[...], preferred_element_type=F32)
    ri = lax.broadcasted_iota(jnp.int32, (tm, 1), 0)
    prev = carry_ref[...]
    g1 = jnp.where(ri == 0, prev[1:2], pltpu.roll(g, 1, 0))
    g2 = jnp.where(ri == 0, prev[0:1], jnp.where(ri == 1, prev[1:2], pltpu.roll(g, 2, 0)))
    cw = cw_ref[...]
    gc = cb_ref[...] + (cw[0:1] * g2 + cw[1:2] * g1 + cw[2:3] * g)
    tail = g[tm - (CONV_WIDTH - 1):tm]
    carry_ref[...] = tail
    conv_o[...] = tail
    y_o[...] = _ffn_tail(x1, gc, u, gf_ref[...], wd_ref[...], lfg_ref[...], lfb_ref[...], alpha=alpha)


def _post_s_body(x_ref, of_ref, od_ref, ga_ref, shf_ref, scf_ref, gf_ref, p0_ref, p1_ref, sg_ref, wo_ref,
                 lag_ref, lab_ref, wg_ref, wu_ref, cw_ref, cb_ref, wd_ref, lfg_ref, lfb_ref, y_o, g_o,
                 *, alpha, lambda_init):
    x1 = _merge_ln(x_ref[...], of_ref[...], od_ref[...], ga_ref[...], sg_ref[...], wo_ref[...],
                   lag_ref[...], lab_ref[...], alpha=alpha, lambda_init=lambda_init)
    h2 = (x1 * (1.0 + scf_ref[...]) + shf_ref[...]).astype(BF16)
    g = jnp.dot(h2, wg_ref[...], preferred_element_type=F32)
    u = jnp.dot(h2, wu_ref[...], preferred_element_type=F32)
    cw = cw_ref[...]
    gc = cb_ref[...] + (cw[0:1] * p0_ref[...] + cw[1:2] * p1_ref[...] + cw[2:3] * g)
    g_o[...] = g
    y_o[...] = _ffn_tail(x1, gc, u, gf_ref[...], wd_ref[...], lfg_ref[...], lfb_ref[...], alpha=alpha)


def _weight_specs(d, dff):
    e = 2 * HEAD_DIM
    return [_whole((1, e)), _whole((d, d)), _whole((1, d)), _whole((1, d)), _whole((d, dff)), _whole((d, dff)),
            _whole((CONV_WIDTH, dff)), _whole((1, dff)), _whole((dff, d)), _whole((1, d)), _whole((1, d))]


def _post_prompt(x, of, od, mods, weights, *, alpha, lambda_init):
    seq, d = x.shape
    dff = weights[4].shape[1]
    tm = min(POST_TILE, seq)
    row = lambda cols: pl.BlockSpec((tm, cols), lambda i: (i, 0))
    body = functools.partial(_post_p_body, alpha=alpha, lambda_init=lambda_init)
    return pl.pallas_call(
        body, grid=(seq // tm,),
        in_specs=[row(d), row(of.shape[1]), row(od.shape[1])] + [_whole((1, d))] * 4 + _weight_specs(d, dff),
        out_specs=(row(d), _whole_out((CONV_WIDTH - 1, dff))),
        out_shape=(jax.ShapeDtypeStruct((seq, d), F32), jax.ShapeDtypeStruct((CONV_WIDTH - 1, dff), F32)),
        scratch_shapes=[pltpu.VMEM((CONV_WIDTH - 1, dff), F32)],
        compiler_params=_params("arbitrary"), name="post_prompt")(x, of, od, *mods, *weights)


def _post_sample(x, of, od, mods, prev0, prev1, weights, *, alpha, lambda_init):
    b, d = x.shape
    dff = weights[4].shape[1]
    body = functools.partial(_post_s_body, alpha=alpha, lambda_init=lambda_init)
    return pl.pallas_call(
        body, grid=(1,),
        in_specs=[_whole((b, d)), _whole(of.shape), _whole(od.shape)] + [_whole((b, d))] * 4
        + [_whole((b, dff))] * 2 + _weight_specs(d, dff),
        out_specs=(_whole_out((b, d)), _whole_out((b, dff))),
        out_shape=(jax.ShapeDtypeStruct((b, d), F32), jax.ShapeDtypeStruct((b, dff), F32)),
        compiler_params=_params("arbitrary"), name="post_sample")(x, of, od, *mods, prev0, prev1, *weights)


def _rotary_n(x, cos, sin):
    half = HEAD_DIM // 2
    out = []
    for g in range(x.shape[1] // HEAD_DIM):
        x1 = x[:, g * HEAD_DIM: g * HEAD_DIM + half]
        x2 = x[:, g * HEAD_DIM + half: (g + 1) * HEAD_DIM]
        out += [x1 * cos - x2 * sin, x1 * sin + x2 * cos]
    return jnp.concatenate(out, axis=1)


def _proj_s_body(x_ref, sh_ref, sc_ref, w_ref, bf_ref, cos_ref, sin_ref,
                 qf_o, kf_o, vf_o, lf_o, qd_o, kd_o, vd_o, *, fw, fh, dw, scale):
    hb = (x_ref[...] * (1.0 + sc_ref[...]) + sh_ref[...]).astype(BF16)
    z = jnp.dot(hb, w_ref[...], preferred_element_type=F32)
    o_qd = 3 * fw
    o_fl = o_qd + 3 * dw
    qf_o[...] = z[:, 0:fw] * scale
    kf_o[...] = z[:, fw:2 * fw]
    vf_o[...] = z[:, 2 * fw:3 * fw]
    lf_o[...] = _log_sigmoid(z[:, o_fl:o_fl + fh] + bf_ref[...])
    qd_o[...] = _rotary_n(z[:, o_qd:o_qd + dw], cos_ref[...], sin_ref[...]) * scale
    kd_o[...] = _rotary_n(z[:, o_qd + dw:o_qd + 2 * dw], cos_ref[...], sin_ref[...])
    vd_o[...] = z[:, o_qd + 2 * dw:o_qd + 3 * dw]


def _proj_sample(x, sh, sc, w_n, b_f, cos, sin, *, fw, fh, dw):
    b, d = x.shape
    shapes = [(b, fw), (b, fw), (b, fw), (b, fh), (b, dw), (b, dw), (b, dw)]
    body = functools.partial(_proj_s_body, fw=fw, fh=fh, dw=dw, scale=HEAD_DIM ** -0.5)
    return pl.pallas_call(
        body, grid=(1,),
        in_specs=[_whole((b, d)), _whole((b, d)), _whole((b, d)), _whole(w_n.shape), _whole((1, fh)),
                  _whole((1, HEAD_DIM // 2)), _whole((1, HEAD_DIM // 2))],
        out_specs=tuple(_whole_out(s) for s in shapes),
        out_shape=tuple(jax.ShapeDtypeStruct(s, F32) for s in shapes),
        compiler_params=_params("arbitrary"), name="proj_sample")(x, sh, sc, w_n, b_f, cos, sin)


def _decode_body(pt_ref, qf_ref, kfs_ref, vfs_ref, lfs_ref, qd_ref, kds_ref, vds_ref, l1q, l1k, l2q, l2k,
                 *rest, n_pages_step, dh, lambda_init):
    g_pages = n_pages_step
    pages = rest[:5 * g_pages]
    of_o, od_o = rest[5 * g_pages:5 * g_pages + 2]
    carry, mf, lf, accf, md, ld, accd = rest[5 * g_pages + 2:]
    c = pl.program_id(1)
    groups, fw = accf.shape
    page = pages[0].shape[2]
    e = 2 * HEAD_DIM

    gi = lax.broadcasted_iota(jnp.int32, (groups, fw), 0)
    li = lax.broadcasted_iota(jnp.int32, (groups, fw), 1)
    own = (li // HEAD_DIM) == gi
    q_f = jnp.where(own, qf_ref[0], 0.0).astype(BF16)
    q_d = jnp.where(own, qd_ref[0], 0.0).astype(BF16)

    @pl.when(c == 0)
    def _():
        carry[...] = jnp.broadcast_to(lfs_ref[0], carry.shape)
        mf[...] = jnp.full_like(mf, NEG)
        md[...] = jnp.full_like(md, NEG)
        lf[...] = jnp.zeros_like(lf)
        ld[...] = jnp.zeros_like(ld)
        accf[...] = jnp.zeros_like(accf)
        accd[...] = jnp.zeros_like(accd)

    fk, fv, lfp, dk, dv = (pages[k * g_pages:(k + 1) * g_pages] for k in range(5))
    rows = g_pages * groups

    def online(m_ref, l_ref, s):
        m_prev = m_ref[:, 0:1]
        m_new = jnp.maximum(m_prev, jnp.max(s, axis=1, keepdims=True))
        alpha = jnp.exp(m_prev - m_new)
        p = jnp.exp(s - m_new)
        l_ref[...] = jnp.broadcast_to(alpha * l_ref[:, 0:1] + jnp.sum(p, axis=1, keepdims=True), l_ref.shape)
        m_ref[...] = jnp.broadcast_to(m_new, m_ref.shape)
        return alpha, p

    def page_cols(v, i):
        return v[:, i * page:(i + 1) * page]

    jj = lax.broadcasted_iota(jnp.int32, (page, 2 * page), 0)
    tt = lax.broadcasted_iota(jnp.int32, (page, 2 * page), 1)
    later_total = jnp.logical_or(tt >= page, jj > tt).astype(BF16)
    lf_rows = jnp.concatenate([r[0] for r in lfp], axis=0)
    sums = jnp.dot(jnp.concatenate(_split3(lf_rows), axis=0), later_total, preferred_element_type=F32)
    sums = sums[0:rows] + sums[rows:2 * rows] + sums[2 * rows:3 * rows]
    after = carry[...]
    s_parts = [None] * g_pages
    for i in reversed(range(g_pages)):
        s_qk = jnp.dot(q_f, fk[i][0].astype(BF16), preferred_element_type=F32)
        s_parts[i] = s_qk + (after + sums[i * groups:(i + 1) * groups, 0:page])
        after = after + sums[i * groups:(i + 1) * groups, page:2 * page]
    carry[...] = after
    alpha, p = online(mf, lf, jnp.concatenate(s_parts, axis=1))
    p = p.astype(BF16)
    pv = sum(lax.dot_general(page_cols(p, i), fv[i][0].astype(BF16), _NT, preferred_element_type=F32)
             for i in range(g_pages))
    accf[...] = alpha * accf[...] + pv

    s = jnp.concatenate([jnp.dot(q_d, dk[i][0].astype(BF16), preferred_element_type=F32)
                         for i in range(g_pages)], axis=1)
    alpha, p = online(md, ld, s)
    p = p.astype(BF16)
    et = lax.broadcasted_iota(jnp.int32, (page, dh * page), 0)
    er = lax.broadcasted_iota(jnp.int32, (page, dh * page), 1)
    spread = ((er // dh) == et).astype(BF16)
    p_rows = jnp.concatenate([page_cols(p, i) for i in range(g_pages)], axis=0)
    pe = jnp.dot(p_rows, spread, preferred_element_type=F32)
    hr = lax.broadcasted_iota(jnp.int32, (rows, dh * page), 1)
    hg = lax.broadcasted_iota(jnp.int32, (rows, dh * page), 0)
    pe = jnp.where((hr % dh) == ((hg % groups) // 2), pe, 0.0).astype(BF16)
    pv = sum(jnp.dot(pe[i * groups:(i + 1) * groups], dv[i][0].astype(BF16), preferred_element_type=F32)
             for i in range(g_pages))
    accd[...] = alpha * accd[...] + pv

    @pl.when(c == pl.num_programs(1) - 1)
    def _():
        rnd = lambda v: v.astype(BF16).astype(F32)
        s = jnp.sum(q_f.astype(F32) * rnd(kfs_ref[0]), axis=1, keepdims=True)
        alpha, p = online(mf, lf, s)
        acc = alpha * accf[...] + rnd(p) * rnd(vfs_ref[0])
        of_o[0] = jnp.sum(jnp.where(own, acc / lf[:, 0:1], 0.0), axis=0, keepdims=True)

        s = jnp.sum(q_d.astype(F32) * rnd(kds_ref[0]), axis=1, keepdims=True)
        alpha, p = online(md, ld, s)
        o = (alpha * accd[...] + rnd(p) * rnd(vds_ref[0])) / ld[:, 0:1]
        lam = _lambda(l1q, l1k, l2q, l2k, lambda_init)
        od_o[0] = jnp.concatenate([o[2 * h:2 * h + 1] - lam * o[2 * h + 1:2 * h + 2] for h in range(dh)], axis=0)


def _decode(page_table, qf, kfs, vfs, lfs, qd, kds, vds, lams, fk, fv, lfp, dk, dv, *, dh, lambda_init):
    b, n_pages = page_table.shape
    fw = qf.shape[-1]
    page = fk.shape[2]
    gp = math.gcd(DECODE_PAGES, n_pages)
    nch = n_pages // gp
    groups = fw // HEAD_DIM
    e = 2 * HEAD_DIM
    per_sample = lambda arr: pl.BlockSpec((1,) + arr.shape[1:], lambda s, c, pt: (s, 0, 0))
    lam_spec = pl.BlockSpec((1, HEAD_DIM), lambda s, c, pt: (0, 0))

    def page_spec(arr, i):
        return pl.BlockSpec((1,) + arr.shape[1:],
                            lambda s, c, pt, i=i: (pt[s * n_pages + (nch - 1 - c) * gp + i], 0, 0))

    page_ops, page_specs = [], []
    for arr in (fk, fv, lfp, dk, dv):
        for i in range(gp):
            page_ops.append(arr)
            page_specs.append(page_spec(arr, i))
    small = (qf, kfs, vfs, lfs, qd, kds, vds)
    body = functools.partial(_decode_body, n_pages_step=gp, dh=dh, lambda_init=lambda_init)
    stat = pltpu.VMEM((groups, LANES), F32)
    return pl.pallas_call(
        body,
        grid_spec=pltpu.PrefetchScalarGridSpec(
            num_scalar_prefetch=1, grid=(b, nch),
            in_specs=[per_sample(a) for a in small] + [lam_spec] * 4 + page_specs,
            out_specs=(pl.BlockSpec((1, 1, fw), lambda s, c, pt: (s, 0, 0)),
                       pl.BlockSpec((1, dh, e), lambda s, c, pt: (s, 0, 0))),
            scratch_shapes=[stat, stat, stat, pltpu.VMEM((groups, fw), F32), stat, stat,
                            pltpu.VMEM((groups, e), F32)]),
        out_shape=(jax.ShapeDtypeStruct((b, 1, fw), F32), jax.ShapeDtypeStruct((b, dh, e), F32)),
        compiler_params=_params("arbitrary", "arbitrary"), name="decode")(
            page_table.reshape(-1), *small, *lams, *page_ops)


def _rope_tables(pos):
    half = HEAD_DIM // 2
    inv = ROPE_THETA ** (-jnp.arange(half, dtype=F32) / half)
    ang = pos.astype(F32)[:, None] * inv[None, :]
    return jnp.cos(ang), jnp.sin(ang)


def kernel(x_prompt, x_sample, cache_fox_k, cache_fox_v, cache_fox_logf, cache_diff_k, cache_diff_v, state_conv, page_table, c_prompt, c_sample, w_ada, b_ada, w_in, b_fgate, lambda_q1, lambda_k1, lambda_q2, lambda_k2, subln_g, w_o, ln_a_g, ln_a_b, w_gate, w_up, conv_w, conv_b, w_down, ln_f_g, ln_f_b):
    batch, seq, d = x_prompt.shape
    b_dec, dec_seq, _ = x_sample.shape
    depth, n_pool, page, fh, _ = cache_fox_k.shape
    dh = cache_diff_k.shape[3]
    fw, dw = fh * HEAD_DIM, dh * 2 * HEAD_DIM
    past_len = page_table.shape[1] * page
    dff = w_gate.shape[-1]
    assert batch == 1 and dec_seq == 1 and fw + dw == d and w_in.shape[-1] == 3 * fw + fh + 3 * dw
    alpha = (2 * depth) ** 0.25

    xp = x_prompt.reshape(seq, d)
    xs = x_sample.reshape(b_dec, d)
    c_all = jnp.concatenate([c_prompt, c_sample], axis=0)
    pad = (-c_all.shape[0]) % SUBLANES
    c_all = jnp.pad(c_all, ((0, pad), (0, 0)))
    cos_p, sin_p = _rope_tables(jnp.arange(seq))
    cos_s, sin_s = _rope_tables(past_len + jnp.arange(dec_seq))
    row2 = lambda v: v.reshape(1, -1)

    new_p = [[] for _ in range(6)]
    new_s = [[] for _ in range(6)]
    for l in range(depth):
        lambda_init = 0.8 - 0.6 * math.exp(-0.3 * l)
        mod = _adaln(c_all, w_ada[l], row2(b_ada[l]))
        mod_p = [mod[0:1, k * d:(k + 1) * d] for k in range(6)]
        mod_s = [mod[1:1 + b_dec, k * d:(k + 1) * d] for k in range(6)]
        lams = tuple(row2(v[l]) for v in (lambda_q1, lambda_k1, lambda_q2, lambda_k2))
        weights = (row2(subln_g[l]), w_o[l].astype(BF16), row2(ln_a_g[l]), row2(ln_a_b[l]),
                   w_gate[l].astype(BF16), w_up[l].astype(BF16), conv_w[l], row2(conv_b[l]),
                   w_down[l].astype(BF16), row2(ln_f_g[l]), row2(ln_f_b[l]))

        w_t = w_in[l].T.astype(BF16)
        (kf_t, vf_t, lf_t, kd_t, vd, qat, ka, vat, qdat, kda, vdat) = _proj_prompt(
            xp, mod_p[0], mod_p[1], w_t, b_fgate[l].reshape(fh, 1), cos_p.T, sin_p.T, fw=fw, fh=fh, dw=dw)
        of, od = _attn_prompt(qat, qdat, ka, kda, vat, vdat, lams, fh=fh, dh=dh, lambda_init=lambda_init)
        xp, conv_new_p = _post_prompt(xp, of, od, mod_p[2:6], weights, alpha=alpha, lambda_init=lambda_init)
        new_p[0].append(jnp.transpose(kf_t.reshape(1, fh, HEAD_DIM, seq), (0, 3, 1, 2)))
        new_p[1].append(jnp.transpose(vf_t.reshape(1, fh, HEAD_DIM, seq), (0, 3, 1, 2)))
        new_p[2].append(jnp.transpose(lf_t.reshape(1, fh, seq), (0, 2, 1)))
        new_p[3].append(jnp.transpose(kd_t.reshape(1, dh, 2, HEAD_DIM, seq), (0, 4, 1, 2, 3)))
        new_p[4].append(vd.reshape(1, seq, dh, 2 * HEAD_DIM))
        new_p[5].append(conv_new_p.reshape(1, CONV_WIDTH - 1, dff))

        wl = w_in[l]
        w_n = jnp.concatenate([wl[:, :3 * fw], wl[:, 3 * fw + fh:], wl[:, 3 * fw:3 * fw + fh]], axis=1).astype(BF16)
        qf, kf, vf, lf, qd, kd, vd_s = _proj_sample(
            xs, mod_s[0], mod_s[1], w_n, row2(b_fgate[l]), cos_s, sin_s, fw=fw, fh=fh, dw=dw)
        fk_v = jnp.transpose(cache_fox_k[l], (0, 2, 3, 1)).reshape(n_pool, fw, page)
        fv_v = jnp.transpose(cache_fox_v[l], (0, 2, 3, 1)).reshape(n_pool, fw, page)
        lf_v = jnp.transpose(cache_fox_logf[l], (0, 2, 1))
        dk_v = jnp.transpose(cache_diff_k[l], (0, 2, 3, 4, 1)).reshape(n_pool, dw, page)
        dv_v = cache_diff_v[l].reshape(n_pool, page * dh, 2 * HEAD_DIM)
        per = lambda a: a.reshape(b_dec, 1, a.shape[-1])
        vds = jnp.repeat(vd_s.reshape(b_dec, dh, 2 * HEAD_DIM), 2, axis=1)
        of_s, od_s = _decode(page_table, per(qf), per(kf), per(vf), lf.reshape(b_dec, fh, 1), per(qd), per(kd), vds,
                             lams, fk_v, fv_v, lf_v, dk_v, dv_v, dh=dh, lambda_init=lambda_init)
        st = state_conv[l]
        xs, g_s = _post_sample(xs, of_s.reshape(b_dec, fw), od_s.reshape(b_dec, dw), mod_s[2:6],
                               st[:, 0], st[:, 1], weights, alpha=alpha, lambda_init=lambda_init)
        new_s[0].append(kf.reshape(b_dec, 1, fh, HEAD_DIM))
        new_s[1].append(vf.reshape(b_dec, 1, fh, HEAD_DIM))
        new_s[2].append(lf.reshape(b_dec, 1, fh))
        new_s[3].append(kd.reshape(b_dec, 1, dh, 2, HEAD_DIM))
        new_s[4].append(vd_s.reshape(b_dec, 1, dh, 2 * HEAD_DIM))
        new_s[5].append(jnp.stack([st[:, 1], g_s], axis=1))

    outs_p = [jnp.stack(a) for a in new_p]
    outs_s = [jnp.stack(a) for a in new_s]
    return (xp.reshape(batch, seq, d), xs.reshape(b_dec, dec_seq, d), *outs_p, *outs_s)
```

```python
import functools
import math

import jax
import jax.numpy as jnp
from jax import lax
from jax.experimental import pallas as pl
from jax.experimental.pallas import tpu as pltpu

F32, BF16 = jnp.float32, jnp.bfloat16

HEAD_DIM = 64
ROPE_THETA = 10000.0
CONV_WIDTH = 3
NEG = -1e30
LOG2E = math.log2(math.e)
LANES = 128
SUBLANES = 8
VMEM_LIMIT_BYTES = 56 * 1024 * 1024

PROJ_TILE = 512
ATTN_TILE = 1024
ATTN_UNIT = 512
POST_TILE = 256
ADALN_TILE = 1024
DECODE_PAGES = 16

FOX_SLOT = 128
VF_SLOT = HEAD_DIM + SUBLANES
VD_SLOT = 2 * HEAD_DIM + SUBLANES

_NT = (((1,), (1,)), ((), ()))


def _params(*sem):
    return pltpu.CompilerParams(dimension_semantics=sem, vmem_limit_bytes=VMEM_LIMIT_BYTES)


def _whole(shape):
    return pl.BlockSpec(shape, lambda *_: (0,) * len(shape), pipeline_mode=pl.Buffered(1))


def _whole_out(shape):
    return pl.BlockSpec(shape, lambda *_: (0,) * len(shape))


def _log_sigmoid(x):
    return jnp.minimum(x, 0.0) - jnp.log1p(jnp.exp(-jnp.abs(x)))


def _split3(x):
    hi = x.astype(BF16)
    r = x - hi.astype(F32)
    mid = r.astype(BF16)
    lo = (r - mid.astype(F32)).astype(BF16)
    return hi, mid, lo


def _dot3(parts, rhs):
    return sum(jnp.dot(p, rhs, preferred_element_type=F32) for p in parts)


def _layer_norm(x, g, b, eps=1e-5):
    mu = jnp.mean(x, axis=-1, keepdims=True)
    xc = x - mu
    var = jnp.mean(xc * xc, axis=-1, keepdims=True)
    return xc * lax.rsqrt(var + eps) * g + b


def _lambda(l1q, l1k, l2q, l2k, lambda_init):
    s1 = jnp.sum(l1q[...] * l1k[...], axis=-1, keepdims=True)
    s2 = jnp.sum(l2q[...] * l2k[...], axis=-1, keepdims=True)
    return jnp.exp(s1) - jnp.exp(s2) + lambda_init


def _adaln_body(c_ref, w_ref, b_ref, o_ref):
    a = jax.nn.silu(c_ref[...]).astype(BF16)
    o_ref[...] = jnp.dot(a, w_ref[...].astype(BF16), preferred_element_type=F32) + b_ref[...]


def _adaln(c_all, w, b):
    rows, d = c_all.shape
    n = w.shape[1]
    tn = min(ADALN_TILE, n)
    return pl.pallas_call(
        _adaln_body, grid=(n // tn,),
        in_specs=[_whole((rows, d)), pl.BlockSpec((d, tn), lambda j: (0, j)), pl.BlockSpec((1, tn), lambda j: (0, j))],
        out_specs=pl.BlockSpec((rows, tn), lambda j: (0, j)),
        out_shape=jax.ShapeDtypeStruct((rows, n), F32),
        compiler_params=_params("arbitrary"), name="adaln")(c_all, w, b)


def _rotary_t(x_t, cos_t, sin_t):
    half = HEAD_DIM // 2
    out = []
    for g in range(x_t.shape[0] // HEAD_DIM):
        x1 = x_t[g * HEAD_DIM: g * HEAD_DIM + half]
        x2 = x_t[g * HEAD_DIM + half: (g + 1) * HEAD_DIM]
        out += [x1 * cos_t - x2 * sin_t, x1 * sin_t + x2 * cos_t]
    return jnp.concatenate(out, axis=0)


def _proj_p_body(x_ref, sh_ref, sc_ref, wt_ref, bf_ref, cos_ref, sin_ref,
                 kft_o, vft_o, lft_o, kdt_o, vd_o, qat_o, ka_o, vat_o, qdat_o, kda_o, vdat_o,
                 carry_ref, *, fw, fh, dw, scale):
    tm = x_ref.shape[0]

    @pl.when(pl.program_id(0) == 0)
    def _():
        carry_ref[...] = jnp.zeros_like(carry_ref)

    hb = (x_ref[...] * (1.0 + sc_ref[...]) + sh_ref[...]).astype(BF16)
    z_t = lax.dot_general(wt_ref[...], hb, _NT, preferred_element_type=F32)
    o_fl = 3 * fw
    o_qd = o_fl + fh
    qf_t = z_t[0:fw] * scale
    kf_t = z_t[fw:2 * fw]
    vf_t = z_t[2 * fw:3 * fw]
    lf_t = _log_sigmoid(z_t[o_fl:o_fl + fh] + bf_ref[...])
    qd_t = _rotary_t(z_t[o_qd:o_qd + dw], cos_ref[...], sin_ref[...]) * scale
    kd_t = _rotary_t(z_t[o_qd + dw:o_qd + 2 * dw], cos_ref[...], sin_ref[...])
    vd_t = z_t[o_qd + 2 * dw:o_qd + 3 * dw]

    j = lax.broadcasted_iota(jnp.int32, (tm, tm), 0)
    t = lax.broadcasted_iota(jnp.int32, (tm, tm), 1)
    c_t = _dot3(_split3(lf_t), (j <= t).astype(BF16)) + carry_ref[:, 0:1]
    carry_ref[...] = jnp.broadcast_to(c_t[:, tm - 1:tm], carry_ref.shape)

    kft_o[...] = kf_t
    vft_o[...] = vf_t
    lft_o[...] = lf_t
    kdt_o[...] = kd_t
    vd_o[...] = vd_t.T

    c_t = c_t * LOG2E
    chi, cmid, clo = (p.astype(F32) for p in _split3(c_t))
    tail_rows = FOX_SLOT - HEAD_DIM
    ri = lax.broadcasted_iota(jnp.int32, (tail_rows, tm), 0)
    qa = []
    for h in range(fh):
        tail = jnp.where(ri == 0, chi[h:h + 1], jnp.where(ri == 1, cmid[h:h + 1], jnp.where(
            ri == 2, clo[h:h + 1], jnp.where(ri < 6, 1.0, 0.0))))
        qa += [qf_t[h * HEAD_DIM:(h + 1) * HEAD_DIM], tail]
    qat_o[...] = jnp.concatenate(qa, axis=0).astype(BF16)

    kf = kf_t.T
    c_n = c_t.T
    nhi, nmid, nlo = (p.astype(F32) for p in _split3(c_n))
    li = lax.broadcasted_iota(jnp.int32, (tm, tail_rows), 1)
    ka = []
    for h in range(fh):
        tail = jnp.where(li < 3, 1.0, jnp.where(li == 3, -nhi[:, h:h + 1], jnp.where(
            li == 4, -nmid[:, h:h + 1], jnp.where(li == 5, -nlo[:, h:h + 1], 0.0))))
        ka += [kf[:, h * HEAD_DIM:(h + 1) * HEAD_DIM], tail]
    ka_o[...] = jnp.concatenate(ka, axis=1).astype(BF16)

    ones_row = (lax.broadcasted_iota(jnp.int32, (SUBLANES, tm), 0) == 0).astype(F32)
    va = []
    for h in range(fh):
        va += [vf_t[h * HEAD_DIM:(h + 1) * HEAD_DIM], ones_row]
    vat_o[...] = jnp.concatenate(va, axis=0).astype(BF16)

    zq = jnp.zeros((FOX_SLOT - HEAD_DIM, tm), F32)
    qda = []
    for g in range(dw // HEAD_DIM):
        qda += [qd_t[g * HEAD_DIM:(g + 1) * HEAD_DIM], zq]
    qdat_o[...] = jnp.concatenate(qda, axis=0).astype(BF16)

    kd = kd_t.T
    zk = jnp.zeros((tm, FOX_SLOT - HEAD_DIM), F32)
    kda = []
    for g in range(dw // HEAD_DIM):
        kda += [kd[:, g * HEAD_DIM:(g + 1) * HEAD_DIM], zk]
    kda_o[...] = jnp.concatenate(kda, axis=1).astype(BF16)

    vda = []
    for h in range(dw // (2 * HEAD_DIM)):
        vda += [vd_t[h * 2 * HEAD_DIM:(h + 1) * 2 * HEAD_DIM], ones_row]
    vdat_o[...] = jnp.concatenate(vda, axis=0).astype(BF16)


def _proj_prompt(x, sh, sc, w_t, b_f, cos_t, sin_t, *, fw, fh, dw):
    seq, d = x.shape
    n_in = w_t.shape[0]
    tm = min(PROJ_TILE, seq)
    dh = dw // (2 * HEAD_DIM)
    ng = dw // HEAD_DIM
    col = lambda rows: pl.BlockSpec((rows, tm), lambda i: (0, i))
    row = lambda cols: pl.BlockSpec((tm, cols), lambda i: (i, 0))
    out_shape = (
        jax.ShapeDtypeStruct((fw, seq), F32), jax.ShapeDtypeStruct((fw, seq), F32),
        jax.ShapeDtypeStruct((fh, seq), F32), jax.ShapeDtypeStruct((dw, seq), F32),
        jax.ShapeDtypeStruct((seq, dw), F32),
        jax.ShapeDtypeStruct((fh * FOX_SLOT, seq), BF16), jax.ShapeDtypeStruct((seq, fh * FOX_SLOT), BF16),
        jax.ShapeDtypeStruct((fh * VF_SLOT, seq), BF16),
        jax.ShapeDtypeStruct((ng * FOX_SLOT, seq), BF16), jax.ShapeDtypeStruct((seq, ng * FOX_SLOT), BF16),
        jax.ShapeDtypeStruct((dh * VD_SLOT, seq), BF16))
    out_specs = (col(fw), col(fw), col(fh), col(dw), row(dw),
                 col(fh * FOX_SLOT), row(fh * FOX_SLOT), col(fh * VF_SLOT),
                 col(ng * FOX_SLOT), row(ng * FOX_SLOT), col(dh * VD_SLOT))
    body = functools.partial(_proj_p_body, fw=fw, fh=fh, dw=dw, scale=HEAD_DIM ** -0.5 * LOG2E)
    return pl.pallas_call(
        body, grid=(seq // tm,),
        in_specs=[row(d), _whole((1, d)), _whole((1, d)), _whole((n_in, d)), _whole((fh, 1)),
                  col(HEAD_DIM // 2), col(HEAD_DIM // 2)],
        out_specs=out_specs, out_shape=out_shape,
        scratch_shapes=[pltpu.VMEM((fh, LANES), F32)],
        compiler_params=_params("arbitrary"), name="proj_prompt")(x, sh, sc, w_t, b_f, cos_t, sin_t)


def _attn_p_body(qi_ref, kj_ref, qat_ref, qdat_ref, ka_ref, kda_ref, vat_ref, vdat_ref,
                 l1q, l1k, l2q, l2k, of_o, od_o, m_ref, accf_ref, accd_ref, *, fh, dh, lambda_init):
    step = pl.program_id(0)
    qi = qi_ref[step]
    kj = kj_ref[step]
    tk = ka_ref.shape[0]
    tq = qat_ref.shape[1]
    tqs = min(ATTN_UNIT, tq)

    @pl.when(kj == 0)
    def _():
        m_ref[...] = jnp.full_like(m_ref, NEG)
        accf_ref[...] = jnp.zeros_like(accf_ref)
        accd_ref[...] = jnp.zeros_like(accd_ref)

    units = [(idx, sub) for idx in range(fh + 2 * dh) for sub in range(tq // tqs)]

    def operands(idx):
        if idx < fh:
            return (ka_ref, qat_ref, idx * FOX_SLOT, vat_ref, idx * VF_SLOT, VF_SLOT, accf_ref, idx)
        g = idx - fh
        return (kda_ref, qdat_ref, g * FOX_SLOT, vdat_ref, (g // 2) * VD_SLOT, VD_SLOT, accd_ref, g)

    def key_extent(sub, masked):
        return min(tk, (sub + 1) * tqs) if masked else tk

    def scores(n, masked):
        idx, sub = units[n]
        k_ref, q_ref, o, _, _, _, _, _ = operands(idx)
        cols = slice(sub * tqs, (sub + 1) * tqs)
        nk = key_extent(sub, masked)
        s_t = jnp.dot(k_ref[0:nk, o:o + FOX_SLOT], q_ref[o:o + FOX_SLOT, cols], preferred_element_type=F32)
        if masked:
            key = lax.broadcasted_iota(jnp.int32, (nk, tqs), 0)
            qry = lax.broadcasted_iota(jnp.int32, (nk, tqs), 1) + sub * tqs
            s_t = jnp.where(key <= qry, s_t, NEG)
        return s_t

    def probs(n, s_t):
        idx, sub = units[n]
        cols = slice(sub * tqs, (sub + 1) * tqs)
        m_prev = m_ref[idx:idx + 1, cols]
        m_new = jnp.maximum(m_prev, jnp.max(s_t, axis=0, keepdims=True))
        return jnp.exp2(s_t - m_new).astype(BF16), m_prev, m_new

    def accumulate(n, p_t, m_prev, m_new):
        idx, sub = units[n]
        _, _, _, v_ref, vo, vs, acc_ref, a = operands(idx)
        cols = slice(sub * tqs, (sub + 1) * tqs)
        pv = jnp.dot(v_ref[vo:vo + vs, 0:p_t.shape[0]], p_t, preferred_element_type=F32)
        acc_ref[a, :, cols] = jnp.exp2(m_prev - m_new) * acc_ref[a, :, cols] + pv
        m_ref[idx:idx + 1, cols] = m_new

    def run(masked):
        s_vals, p_vals = {}, {}
        for n in range(len(units) + 2):
            if n < len(units):
                s_vals[n] = scores(n, masked)
            if 0 <= n - 1 < len(units):
                p_vals[n - 1] = probs(n - 1, s_vals.pop(n - 1))
            if 0 <= n - 2 < len(units):
                accumulate(n - 2, *p_vals.pop(n - 2))

    @pl.when(kj < qi)
    def _():
        run(masked=False)

    @pl.when(kj == qi)
    def _():
        run(masked=True)
        outs = []
        for h in range(fh):
            a = accf_ref[h]
            outs.append(a[0:HEAD_DIM] / a[HEAD_DIM:HEAD_DIM + 1])
        of_o[...] = jnp.concatenate(outs, axis=0).T.astype(BF16)
        lam = _lambda(l1q, l1k, l2q, l2k, lambda_init)
        e = 2 * HEAD_DIM
        outs = []
        for h in range(dh):
            a1 = accd_ref[2 * h]
            a2 = accd_ref[2 * h + 1]
            outs.append(a1[0:e] / a1[e:e + 1] - lam * (a2[0:e] / a2[e:e + 1]))
        od_o[...] = jnp.concatenate(outs, axis=0).T


def _attn_prompt(qat, qdat, ka, kda, vat, vdat, lams, *, fh, dh, lambda_init):
    seq = ka.shape[0]
    t = min(ATTN_TILE, seq)
    nb = seq // t
    pairs = [(q, k) for q in range(nb) for k in range(q + 1)]
    qi_tab = jnp.asarray([p[0] for p in pairs], jnp.int32)
    kj_tab = jnp.asarray([p[1] for p in pairs], jnp.int32)
    qcol = lambda rows: pl.BlockSpec((rows, t), lambda s, qi, kj: (0, qi[s]))
    kcol = lambda rows: pl.BlockSpec((rows, t), lambda s, qi, kj: (0, kj[s]))
    krow = lambda cols: pl.BlockSpec((t, cols), lambda s, qi, kj: (kj[s], 0))
    orow = lambda cols: pl.BlockSpec((t, cols), lambda s, qi, kj: (qi[s], 0))
    lam_spec = pl.BlockSpec((1, HEAD_DIM), lambda s, qi, kj: (0, 0))
    fw, dw = fh * HEAD_DIM, dh * 2 * HEAD_DIM
    body = functools.partial(_attn_p_body, fh=fh, dh=dh, lambda_init=lambda_init)
    return pl.pallas_call(
        body,
        grid_spec=pltpu.PrefetchScalarGridSpec(
            num_scalar_prefetch=2, grid=(len(pairs),),
            in_specs=[qcol(qat.shape[0]), qcol(qdat.shape[0]), krow(ka.shape[1]), krow(kda.shape[1]),
                      kcol(vat.shape[0]), kcol(vdat.shape[0]), lam_spec, lam_spec, lam_spec, lam_spec],
            out_specs=(orow(fw), orow(dw)),
            scratch_shapes=[pltpu.VMEM((fh + 2 * dh, t), F32), pltpu.VMEM((fh, VF_SLOT, t), F32),
                            pltpu.VMEM((2 * dh, VD_SLOT, t), F32)]),
        out_shape=(jax.ShapeDtypeStruct((seq, fw), BF16), jax.ShapeDtypeStruct((seq, dw), F32)),
        compiler_params=_params("arbitrary"), name="attn_prompt")(qi_tab, kj_tab, qat, qdat, ka, kda, vat, vdat, *lams)


def _merge_ln(x, of, od, g_a, subln_g, w_o, ln_g, ln_b, *, alpha, lambda_init):
    e = 2 * HEAD_DIM
    parts = [of.astype(BF16)]
    for h in range(od.shape[1] // e):
        seg = od[:, h * e:(h + 1) * e]
        y = seg * lax.rsqrt(jnp.mean(seg * seg, axis=-1, keepdims=True) + 1e-6) * subln_g
        parts.append((y * (1.0 - lambda_init)).astype(BF16))
    mix = jnp.dot(jnp.concatenate(parts, axis=1), w_o, preferred_element_type=F32)
    return _layer_norm(alpha * x + g_a * mix, ln_g, ln_b)


def _ffn_tail(x1, gc, u, g_f, w_down, ln_g, ln_b, *, alpha):
    act = (jax.nn.silu(gc) * u).astype(BF16)
    f = jnp.dot(act, w_down, preferred_element_type=F32)
    return _layer_norm(alpha * x1 + g_f * f, ln_g, ln_b)


def _post_p_body(x_ref, of_ref, od_ref, ga_ref, shf_ref, scf_ref, gf_ref, sg_ref, wo_ref, lag_ref, lab_ref,
                 wg_ref, wu_ref, cw_ref, cb_ref, wd_ref, lfg_ref, lfb_ref, y_o, conv_o, carry_ref,
                 *, alpha, lambda_init):
    tm = x_ref.shape[0]

    @pl.when(pl.program_id(0) == 0)
    def _():
        carry_ref[...] = jnp.zeros_like(carry_ref)

    x1 = _merge_ln(x_ref[...], of_ref[...], od_ref[...], ga_ref[...], sg_ref[...], wo_ref[...],
                   lag_ref[...], lab_ref[...], alpha=alpha, lambda_init=lambda_init)
    h2 = (x1 * (1.0 + scf_ref[...]) + shf_ref[...]).astype(BF16)
    g = jnp.dot(h2, wg_ref[...], preferred_element_type=F32)
    u = jnp.dot(h2, wu_ref[...], preferred_element_type=F32)
    ri = lax.broadcasted_iota(jnp.int32, (tm, 1), 0)
    prev = carry_ref[...]
    g1 = jnp.where(ri == 0, prev[1:2], pltpu.roll(g, 1, 0))
    g2 = jnp.where(ri == 0, prev[0:1], jnp.where(ri == 1, prev[1:2], pltpu.roll(g, 2, 0)))
    cw = cw_ref[...]
    gc = cb_ref[...] + (cw[0:1] * g2 + cw[1:2] * g1 + cw[2:3] * g)
    tail = g[tm - (CONV_WIDTH - 1):tm]
    carry_ref[...] = tail
    conv_o[...] = tail
    y_o[...] = _ffn_tail(x1, gc, u, gf_ref[...], wd_ref[...], lfg_ref[...], lfb_ref[...], alpha=alpha)


def _post_s_body(x_ref, of_ref, od_ref, ga_ref, shf_ref, scf_ref, gf_ref, p0_ref, p1_ref, sg_ref, wo_ref,
                 lag_ref, lab_ref, wg_ref, wu_ref, cw_ref, cb_ref, wd_ref, lfg_ref, lfb_ref, y_o, g_o,
                 *, alpha, lambda_init):
    x1 = _merge_ln(x_ref[...], of_ref[...], od_ref[...], ga_ref[...], sg_ref[...], wo_ref[...],
                   lag_ref[...], lab_ref[...], alpha=alpha, lambda_init=lambda_init)
    h2 = (x1 * (1.0 + scf_ref[...]) + shf_ref[...]).astype(BF16)
    g = jnp.dot(h2, wg_ref[...], preferred_element_type=F32)
    u = jnp.dot(h2, wu_ref[...], preferred_element_type=F32)
    cw = cw_ref[...]
    gc = cb_ref[...] + (cw[0:1] * p0_ref[...] + cw[1:2] * p1_ref[...] + cw[2:3] * g)
    g_o[...] = g
    y_o[...] = _ffn_tail(x1, gc, u, gf_ref[...], wd_ref[...], lfg_ref[...], lfb_ref[...], alpha=alpha)


def _weight_specs(d, dff):
    e = 2 * HEAD_DIM
    return [_whole((1, e)), _whole((d, d)), _whole((1, d)), _whole((1, d)), _whole((d, dff)), _whole((d, dff)),
            _whole((CONV_WIDTH, dff)), _whole((1, dff)), _whole((dff, d)), _whole((1, d)), _whole((1, d))]


def _post_prompt(x, of, od, mods, weights, *, alpha, lambda_init):
    seq, d = x.shape
    dff = weights[4].shape[1]
    tm = min(POST_TILE, seq)
    row = lambda cols: pl.BlockSpec((tm, cols), lambda i: (i, 0))
    body = functools.partial(_post_p_body, alpha=alpha, lambda_init=lambda_init)
    return pl.pallas_call(
        body, grid=(seq // tm,),
        in_specs=[row(d), row(of.shape[1]), row(od.shape[1])] + [_whole((1, d))] * 4 + _weight_specs(d, dff),
        out_specs=(row(d), _whole_out((CONV_WIDTH - 1, dff))),
        out_shape=(jax.ShapeDtypeStruct((seq, d), F32), jax.ShapeDtypeStruct((CONV_WIDTH - 1, dff), F32)),
        scratch_shapes=[pltpu.VMEM((CONV_WIDTH - 1, dff), F32)],
        compiler_params=_params("arbitrary"), name="post_prompt")(x, of, od, *mods, *weights)


def _post_sample(x, of, od, mods, prev0, prev1, weights, *, alpha, lambda_init):
    b, d = x.shape
    dff = weights[4].shape[1]
    body = functools.partial(_post_s_body, alpha=alpha, lambda_init=lambda_init)
    return pl.pallas_call(
        body, grid=(1,),
        in_specs=[_whole((b, d)), _whole(of.shape), _whole(od.shape)] + [_whole((b, d))] * 4
        + [_whole((b, dff))] * 2 + _weight_specs(d, dff),
        out_specs=(_whole_out((b, d)), _whole_out((b, dff))),
        out_shape=(jax.ShapeDtypeStruct((b, d), F32), jax.ShapeDtypeStruct((b, dff), F32)),
        compiler_params=_params("arbitrary"), name="post_sample")(x, of, od, *mods, prev0, prev1, *weights)


def _rotary_n(x, cos, sin):
    half = HEAD_DIM // 2
    out = []
    for g in range(x.shape[1] // HEAD_DIM):
        x1 = x[:, g * HEAD_DIM: g * HEAD_DIM + half]
        x2 = x[:, g * HEAD_DIM + half: (g + 1) * HEAD_DIM]
        out += [x1 * cos - x2 * sin, x1 * sin + x2 * cos]
    return jnp.concatenate(out, axis=1)


def _proj_s_body(x_ref, sh_ref, sc_ref, w_ref, bf_ref, cos_ref, sin_ref,
                 qf_o, kf_o, vf_o, lf_o, qd_o, kd_o, vd_o, *, fw, fh, dw, scale):
    hb = (x_ref[...] * (1.0 + sc_ref[...]) + sh_ref[...]).astype(BF16)
    z = jnp.dot(hb, w_ref[...], preferred_element_type=F32)
    o_qd = 3 * fw
    o_fl = o_qd + 3 * dw
    qf_o[...] = z[:, 0:fw] * scale
    kf_o[...] = z[:, fw:2 * fw]
    vf_o[...] = z[:, 2 * fw:3 * fw]
    lf_o[...] = _log_sigmoid(z[:, o_fl:o_fl + fh] + bf_ref[...])
    qd_o[...] = _rotary_n(z[:, o_qd:o_qd + dw], cos_ref[...], sin_ref[...]) * scale
    kd_o[...] = _rotary_n(z[:, o_qd + dw:o_qd + 2 * dw], cos_ref[...], sin_ref[...])
    vd_o[...] = z[:, o_qd + 2 * dw:o_qd + 3 * dw]


def _proj_sample(x, sh, sc, w_n, b_f, cos, sin, *, fw, fh, dw):
    b, d = x.shape
    shapes = [(b, fw), (b, fw), (b, fw), (b, fh), (b, dw), (b, dw), (b, dw)]
    body = functools.partial(_proj_s_body, fw=fw, fh=fh, dw=dw, scale=HEAD_DIM ** -0.5)
    return pl.pallas_call(
        body, grid=(1,),
        in_specs=[_whole((b, d)), _whole((b, d)), _whole((b, d)), _whole(w_n.shape), _whole((1, fh)),
                  _whole((1, HEAD_DIM // 2)), _whole((1, HEAD_DIM // 2))],
        out_specs=tuple(_whole_out(s) for s in shapes),
        out_shape=tuple(jax.ShapeDtypeStruct(s, F32) for s in shapes),
        compiler_params=_params("arbitrary"), name="proj_sample")(x, sh, sc, w_n, b_f, cos, sin)


def _decode_body(pt_ref, qf_ref, kfs_ref, vfs_ref, lfs_ref, qd_ref, kds_ref, vds_ref, l1q, l1k, l2q, l2k,
                 *rest, n_pages_step, dh, lambda_init):
    g_pages = n_pages_step
    pages = rest[:5 * g_pages]
    of_o, od_o = rest[5 * g_pages:5 * g_pages + 2]
    carry, mf, lf, accf, md, ld, accd = rest[5 * g_pages + 2:]
    c = pl.program_id(1)
    groups, fw = accf.shape
    page = pages[0].shape[2]
    e = 2 * HEAD_DIM

    gi = lax.broadcasted_iota(jnp.int32, (groups, fw), 0)
    li = lax.broadcasted_iota(jnp.int32, (groups, fw), 1)
    own = (li // HEAD_DIM) == gi
    q_f = jnp.where(own, qf_ref[0], 0.0).astype(BF16)
    q_d = jnp.where(own, qd_ref[0], 0.0).astype(BF16)

    @pl.when(c == 0)
    def _():
        carry[...] = jnp.broadcast_to(lfs_ref[0], carry.shape)
        mf[...] = jnp.full_like(mf, NEG)
        md[...] = jnp.full_like(md, NEG)
        lf[...] = jnp.zeros_like(lf)
        ld[...] = jnp.zeros_like(ld)
        accf[...] = jnp.zeros_like(accf)
        accd[...] = jnp.zeros_like(accd)

    fk, fv, lfp, dk, dv = (pages[k * g_pages:(k + 1) * g_pages] for k in range(5))
    rows = g_pages * groups

    def online(m_ref, l_ref, s):
        m_prev = m_ref[:, 0:1]
        m_new = jnp.maximum(m_prev, jnp.max(s, axis=1, keepdims=True))
        alpha = jnp.exp(m_prev - m_new)
        p = jnp.exp(s - m_new)
        l_ref[...] = jnp.broadcast_to(alpha * l_ref[:, 0:1] + jnp.sum(p, axis=1, keepdims=True), l_ref.shape)
        m_ref[...] = jnp.broadcast_to(m_new, m_ref.shape)
        return alpha, p

    def page_cols(v, i):
        return v[:, i * page:(i + 1) * page]

    jj = lax.broadcasted_iota(jnp.int32, (page, 2 * page), 0)
    tt = lax.broadcasted_iota(jnp.int32, (page, 2 * page), 1)
    later_total = jnp.logical_or(tt >= page, jj > tt).astype(BF16)
    lf_rows = jnp.concatenate([r[0] for r in lfp], axis=0)
    sums = jnp.dot(jnp.concatenate(_split3(lf_rows), axis=0), later_total, preferred_element_type=F32)
    sums = sums[0:rows] + sums[rows:2 * rows] + sums[2 * rows:3 * rows]
    after = carry[...]
    s_parts = [None] * g_pages
    for i in reversed(range(g_pages)):
        s_qk = jnp.dot(q_f, fk[i][0].astype(BF16), preferred_element_type=F32)
        s_parts[i] = s_qk + (after + sums[i * groups:(i + 1) * groups, 0:page])
        after = after + sums[i * groups:(i + 1) * groups, page:2 * page]
    carry[...] = after
    alpha, p = online(mf, lf, jnp.concatenate(s_parts, axis=1))
    p = p.astype(BF16)
    pv = sum(lax.dot_general(page_cols(p, i), fv[i][0].astype(BF16), _NT, preferred_element_type=F32)
             for i in range(g_pages))
    accf[...] = alpha * accf[...] + pv

    s = jnp.concatenate([jnp.dot(q_d, dk[i][0].astype(BF16), preferred_element_type=F32)
                         for i in range(g_pages)], axis=1)
    alpha, p = online(md, ld, s)
    p = p.astype(BF16)
    et = lax.broadcasted_iota(jnp.int32, (page, dh * page), 0)
    er = lax.broadcasted_iota(jnp.int32, (page, dh * page), 1)
    spread = ((er // dh) == et).astype(BF16)
    p_rows = jnp.concatenate([page_cols(p, i) for i in range(g_pages)], axis=0)
    pe = jnp.dot(p_rows, spread, preferred_element_type=F32)
    hr = lax.broadcasted_iota(jnp.int32, (rows, dh * page), 1)
    hg = lax.broadcasted_iota(jnp.int32, (rows, dh * page), 0)
    pe = jnp.where((hr % dh) == ((hg % groups) // 2), pe, 0.0).astype(BF16)
    pv = sum(jnp.dot(pe[i * groups:(i + 1) * groups], dv[i][0].astype(BF16), preferred_element_type=F32)
             for i in range(g_pages))
    accd[...] = alpha * accd[...] + pv

    @pl.when(c == pl.num_programs(1) - 1)
    def _():
        rnd = lambda v: v.astype(BF16).astype(F32)
        s = jnp.sum(q_f.astype(F32) * rnd(kfs_ref[0]), axis=1, keepdims=True)
        alpha, p = online(mf, lf, s)
        acc = alpha * accf[...] + rnd(p) * rnd(vfs_ref[0])
        of_o[0] = jnp.sum(jnp.where(own, acc / lf[:, 0:1], 0.0), axis=0, keepdims=True)

        s = jnp.sum(q_d.astype(F32) * rnd(kds_ref[0]), axis=1, keepdims=True)
        alpha, p = online(md, ld, s)
        o = (alpha * accd[...] + rnd(p) * rnd(vds_ref[0])) / ld[:, 0:1]
        lam = _lambda(l1q, l1k, l2q, l2k, lambda_init)
        od_o[0] = jnp.concatenate([o[2 * h:2 * h + 1] - lam * o[2 * h + 1:2 * h + 2] for h in range(dh)], axis=0)


def _decode(page_table, qf, kfs, vfs, lfs, qd, kds, vds, lams, fk, fv, lfp, dk, dv, *, dh, lambda_init):
    b, n_pages = page_table.shape
    fw = qf.shape[-1]
    page = fk.shape[2]
    gp = math.gcd(DECODE_PAGES, n_pages)
    nch = n_pages // gp
    groups = fw // HEAD_DIM
    e = 2 * HEAD_DIM
    per_sample = lambda arr: pl.BlockSpec((1,) + arr.shape[1:], lambda s, c, pt: (s, 0, 0))
    lam_spec = pl.BlockSpec((1, HEAD_DIM), lambda s, c, pt: (0, 0))

    def page_spec(arr, i):
        return pl.BlockSpec((1,) + arr.shape[1:],
                            lambda s, c, pt, i=i: (pt[s * n_pages + (nch - 1 - c) * gp + i], 0, 0))

    page_ops, page_specs = [], []
    for arr in (fk, fv, lfp, dk, dv):
        for i in range(gp):
            page_ops.append(arr)
            page_specs.append(page_spec(arr, i))
    small = (qf, kfs, vfs, lfs, qd, kds, vds)
    body = functools.partial(_decode_body, n_pages_step=gp, dh=dh, lambda_init=lambda_init)
    stat = pltpu.VMEM((groups, LANES), F32)
    return pl.pallas_call(
        body,
        grid_spec=pltpu.PrefetchScalarGridSpec(
            num_scalar_prefetch=1, grid=(b, nch),
            in_specs=[per_sample(a) for a in small] + [lam_spec] * 4 + page_specs,
            out_specs=(pl.BlockSpec((1, 1, fw), lambda s, c, pt: (s, 0, 0)),
                       pl.BlockSpec((1, dh, e), lambda s, c, pt: (s, 0, 0))),
            scratch_shapes=[stat, stat, stat, pltpu.VMEM((groups, fw), F32), stat, stat,
                            pltpu.VMEM((groups, e), F32)]),
        out_shape=(jax.ShapeDtypeStruct((b, 1, fw), F32), jax.ShapeDtypeStruct((b, dh, e), F32)),
        compiler_params=_params("arbitrary", "arbitrary"), name="decode")(
            page_table.reshape(-1), *small, *lams, *page_ops)


def _rope_tables(pos):
    half = HEAD_DIM // 2
    inv = ROPE_THETA ** (-jnp.arange(half, dtype=F32) / half)
    ang = pos.astype(F32)[:, None] * inv[None, :]
    return jnp.cos(ang), jnp.sin(ang)


def kernel(x_prompt, x_sample, cache_fox_k, cache_fox_v, cache_fox_logf, cache_diff_k, cache_diff_v, state_conv, page_table, c_prompt, c_sample, w_ada, b_ada, w_in, b_fgate, lambda_q1, lambda_k1, lambda_q2, lambda_k2, subln_g, w_o, ln_a_g, ln_a_b, w_gate, w_up, conv_w, conv_b, w_down, ln_f_g, ln_f_b):
    batch, seq, d = x_prompt.shape
    b_dec, dec_seq, _ = x_sample.shape
    depth, n_pool, page, fh, _ = cache_fox_k.shape
    dh = cache_diff_k.shape[3]
    fw, dw = fh * HEAD_DIM, dh * 2 * HEAD_DIM
    past_len = page_table.shape[1] * page
    dff = w_gate.shape[-1]
    assert batch == 1 and dec_seq == 1 and fw + dw == d and w_in.shape[-1] == 3 * fw + fh + 3 * dw
    alpha = (2 * depth) ** 0.25

    xp = x_prompt.reshape(seq, d)
    xs = x_sample.reshape(b_dec, d)
    c_all = jnp.concatenate([c_prompt, c_sample], axis=0)
    pad = (-c_all.shape[0]) % SUBLANES
    c_all = jnp.pad(c_all, ((0, pad), (0, 0)))
    cos_p, sin_p = _rope_tables(jnp.arange(seq))
    cos_s, sin_s = _rope_tables(past_len + jnp.arange(dec_seq))
    row2 = lambda v: v.reshape(1, -1)

    new_p = [[] for _ in range(6)]
    new_s = [[] for _ in range(6)]
    for l in range(depth):
        lambda_init = 0.8 - 0.6 * math.exp(-0.3 * l)
        mod = _adaln(c_all, w_ada[l], row2(b_ada[l]))
        mod_p = [mod[0:1, k * d:(k + 1) * d] for k in range(6)]
        mod_s = [mod[1:1 + b_dec, k * d:(k + 1) * d] for k in range(6)]
        lams = tuple(row2(v[l]) for v in (lambda_q1, lambda_k1, lambda_q2, lambda_k2))
        weights = (row2(subln_g[l]), w_o[l].astype(BF16), row2(ln_a_g[l]), row2(ln_a_b[l]),
                   w_gate[l].astype(BF16), w_up[l].astype(BF16), conv_w[l], row2(conv_b[l]),
                   w_down[l].astype(BF16), row2(ln_f_g[l]), row2(ln_f_b[l]))

        w_t = w_in[l].T.astype(BF16)
        (kf_t, vf_t, lf_t, kd_t, vd, qat, ka, vat, qdat, kda, vdat) = _proj_prompt(
            xp, mod_p[0], mod_p[1], w_t, b_fgate[l].reshape(fh, 1), cos_p.T, sin_p.T, fw=fw, fh=fh, dw=dw)
        of, od = _attn_prompt(qat, qdat, ka, kda, vat, vdat, lams, fh=fh, dh=dh, lambda_init=lambda_init)
        xp, conv_new_p = _post_prompt(xp, of, od, mod_p[2:6], weights, alpha=alpha, lambda_init=lambda_init)
        new_p[0].append(jnp.transpose(kf_t.reshape(1, fh, HEAD_DIM, seq), (0, 3, 1, 2)))
        new_p[1].append(jnp.transpose(vf_t.reshape(1, fh, HEAD_DIM, seq), (0, 3, 1, 2)))
        new_p[2].append(jnp.transpose(lf_t.reshape(1, fh, seq), (0, 2, 1)))
        new_p[3].append(jnp.transpose(kd_t.reshape(1, dh, 2, HEAD_DIM, seq), (0, 4, 1, 2, 3)))
        new_p[4].append(vd.reshape(1, seq, dh, 2 * HEAD_DIM))
        new_p[5].append(conv_new_p.reshape(1, CONV_WIDTH - 1, dff))

        wl = w_in[l]
        w_n = jnp.concatenate([wl[:, :3 * fw], wl[:, 3 * fw + fh:], wl[:, 3 * fw:3 * fw + fh]], axis=1).astype(BF16)
        qf, kf, vf, lf, qd, kd, vd_s = _proj_sample(
            xs, mod_s[0], mod_s[1], w_n, row2(b_fgate[l]), cos_s, sin_s, fw=fw, fh=fh, dw=dw)
        fk_v = jnp.transpose(cache_fox_k[l], (0, 2, 3, 1)).reshape(n_pool, fw, page)
        fv_v = jnp.transpose(cache_fox_v[l], (0, 2, 3, 1)).reshape(n_pool, fw, page)
        lf_v = jnp.transpose(cache_fox_logf[l], (0, 2, 1))
        dk_v = jnp.transpose(cache_diff_k[l], (0, 2, 3, 4, 1)).reshape(n_pool, dw, page)
        dv_v = cache_diff_v[l].reshape(n_pool, page * dh, 2 * HEAD_DIM)
        per = lambda a: a.reshape(b_dec, 1, a.shape[-1])
        vds = jnp.repeat(vd_s.reshape(b_dec, dh, 2 * HEAD_DIM), 2, axis=1)
        of_s, od_s = _decode(page_table, per(qf), per(kf), per(vf), lf.reshape(b_dec, fh, 1), per(qd), per(kd), vds,
                             lams, fk_v, fv_v, lf_v, dk_v, dv_v, dh=dh, lambda_init=lambda_init)
        st = state_conv[l]
        xs, g_s = _post_sample(xs, of_s.reshape(b_dec, fw), od_s.reshape(b_dec, dw), mod_s[2:6],
                               st[:, 0], st[:, 1], weights, alpha=alpha, lambda_init=lambda_init)
        new_s[0].append(kf.reshape(b_dec, 1, fh, HEAD_DIM))
        new_s[1].append(vf.reshape(b_dec, 1, fh, HEAD_DIM))
        new_s[2].append(lf.reshape(b_dec, 1, fh))
        new_s[3].append(kd.reshape(b_dec, 1, dh, 2, HEAD_DIM))
        new_s[4].append(vd_s.reshape(b_dec, 1, dh, 2 * HEAD_DIM))
        new_s[5].append(jnp.stack([st[:, 1], g_s], axis=1))

    outs_p = [jnp.stack(a) for a in new_p]
    outs_s = [jnp.stack(a) for a in new_s]
    return (xp.reshape(batch, seq, d), xs.reshape(b_dec, dec_seq, d), *outs_p, *outs_s)
```
